```python
import math
import jax, jax.numpy as jnp
from jax import lax
import numpy as np

D_MODEL = 1024
BATCH = 1
SEQ = 16384
DEPTH = 1

HEAD_DIM = 64
D_ATTN = D_MODEL // 2
N_ATTN_HEADS = D_ATTN // HEAD_DIM
D_RNN = D_MODEL - D_ATTN
LRU_BLOCK = 64
N_LRU_BLOCKS = D_RNN // LRU_BLOCK
D_MIX = D_ATTN + D_RNN
D_IN_PROJ = 3 * D_ATTN + 2 * D_RNN
DILATED_PATTERNS = ((128, 1), (512, 4), (2048, 16))
ATTN_BLOCK = 128
CONV_WIDTH = 4
LRU_C = 8.0
D_FF = 2816
FFN_RES_WEIGHT = 0.5
N_SUBLAYERS = 3
RMS_EPS = 1e-6

kernel_name = "hybrid_dilated_attn_rglru_macaron_block"


def _rms(x):
    xf = x.astype(jnp.float32)
    return xf * lax.rsqrt(jnp.mean(xf * xf, axis=-1, keepdims=True) + RMS_EPS)


def _pre(x, g, shift, scale):
    return (_rms(x) * g * (1.0 + scale[:, None, :]) + shift[:, None, :]).astype(x.dtype)


def _post_residual(x, y, g, gate, weight):
    return x + (weight * (1.0 + gate[:, None, :]) * _rms(y) * g).astype(x.dtype)


def _swiglu(h, w_in, w_out):
    hg, hu = jnp.split(h @ w_in, 2, axis=-1)
    return (jax.nn.silu(hg) * hu) @ w_out


def _dilated_band(q, k, v, window, dilation):
    B, S, H, Dh = q.shape
    n_back = window // dilation
    chunk = dilation * ATTN_BLOCK
    s_pad = -(-S // chunk) * chunk
    L = s_pad // dilation
    nb = L // ATTN_BLOCK

    def to_strided(t):
        t = jnp.pad(t, ((0, 0), (0, s_pad - S), (0, 0), (0, 0)))
        t = t.reshape(B, L, dilation, H, Dh).transpose(0, 2, 1, 3, 4)
        return t.reshape(B, dilation, nb, ATTN_BLOCK, H, Dh)

    qs, ks, vs = to_strided(q), to_strided(k), to_strided(v)

    def with_prev(t):
        prev = jnp.pad(t[:, :, :-1], ((0, 0), (0, 0), (1, 0), (0, 0), (0, 0), (0, 0)))
        return jnp.concatenate([prev, t], axis=3)

    kk, vv = with_prev(ks), with_prev(vs)
    scores = jnp.einsum('brnqhd,brnkhd->brnhqk', qs, kk,
                        preferred_element_type=jnp.float32) * (Dh ** -0.5)
    qi = jnp.arange(ATTN_BLOCK)[:, None] + ATTN_BLOCK
    kj = jnp.arange(2 * ATTN_BLOCK)[None, :]
    dist = qi - kj
    band = (dist >= 0) & (dist <= n_back)
    has_prev = (jnp.arange(nb) > 0)[:, None, None] | (kj >= ATTN_BLOCK)[None]
    mask = (band[None] & has_prev)[:, None]
    scores = jnp.where(mask, scores, -jnp.inf)
    m = jnp.max(scores, axis=-1)
    p = jnp.exp(scores - m[..., None])
    l = jnp.sum(p, axis=-1)
    o = jnp.einsum('brnhqk,brnkhd->brnqhd', p, vv.astype(jnp.float32))

    def from_strided(t):
        X = t.shape[-1]
        t = t.reshape(B, dilation, L, H, X).transpose(0, 2, 1, 3, 4).reshape(B, s_pad, H, X)
        return t[:, :S]

    stats = from_strided(jnp.swapaxes(jnp.stack([m, l], axis=-1), 3, 4))
    return from_strided(o), stats[..., 0], stats[..., 1]


def _dilated_attention(q, k, v):
    res = [_dilated_band(q, k, v, w, d) for (w, d) in DILATED_PATTERNS]
    m_max = jnp.max(jnp.stack([r[1] for r in res]), axis=0)
    rescale = [jnp.exp(r[1] - m_max) for r in res]
    num = res[0][0] * rescale[0][..., None]
    den = res[0][2] * rescale[0]
    for r, s in zip(res[1:], rescale[1:]):
        num = num + r[0] * s[..., None]
        den = den + r[2] * s
    return num / den[..., None]


def _blockdiag(x, w):
    xb = x.reshape(x.shape[:-1] + (w.shape[0], w.shape[1]))
    return jnp.einsum('bsgi,gij->bsgj', xb, w).reshape(x.shape)


def _rg_lru(x, w_x, b_x, w_a, b_a, lam):
    f32 = jnp.float32
    xf = x.astype(f32)
    gate_x = jax.nn.sigmoid(_blockdiag(xf, w_x.astype(f32)) + b_x.astype(f32))
    gate_a = jax.nn.sigmoid(_blockdiag(xf, w_a.astype(f32)) + b_a.astype(f32))
    log_a = -LRU_C * gate_a * jax.nn.softplus(-lam.astype(f32))
    a = jnp.exp(log_a)
    b = xf * gate_x * jnp.sqrt(-jnp.expm1(2.0 * log_a))

    def combine(lhs, rhs):
        a_l, b_l = lhs
        a_r, b_r = rhs
        return a_l * a_r, a_r * b_l + b_r

    _, h = lax.associative_scan(combine, (a, b), axis=1)
    return h


def _hybrid_mixer(h, w_in, conv_w, conv_b, lru_w_x, lru_b_x, lru_w_a, lru_b_a, lru_lambda, w_out):
    B, S, _ = h.shape
    z = h @ w_in
    q, k, v, xr, gr = jnp.split(
        z, [D_ATTN, 2 * D_ATTN, 3 * D_ATTN, 3 * D_ATTN + D_RNN], axis=-1)
    heads = lambda t: t.reshape(B, S, N_ATTN_HEADS, HEAD_DIM)
    attn = _dilated_attention(heads(q), heads(k), heads(v)).reshape(B, S, D_ATTN)
    xr = lax.conv_general_dilated(
        xr, conv_w[:, None, :], window_strides=(1,), padding=[(CONV_WIDTH - 1, 0)],
        dimension_numbers=('NWC', 'WIO', 'NWC'), feature_group_count=D_RNN) + conv_b
    rec = _rg_lru(xr, lru_w_x, lru_b_x, lru_w_a, lru_b_a, lru_lambda) \
        * jax.nn.gelu(gr.astype(jnp.float32), approximate=True)
    mixed = jnp.concatenate([attn, rec], axis=-1).astype(h.dtype)
    return mixed @ w_out


def setup_inputs(seed: int = 0) -> dict:
    key = jax.random.key(seed)
    ks = jax.random.split(key, 20)
    f32 = jnp.float32
    nrm = lambda k, shape, fan_in: jax.random.normal(k, shape, f32) * (fan_in ** -0.5)
    u = jax.random.uniform(ks[17], (DEPTH, D_RNN), f32, minval=0.9, maxval=0.999)
    return {
        "x": jax.random.normal(ks[0], (BATCH, SEQ, D_MODEL), f32),
        "c": jax.random.normal(ks[1], (BATCH, D_MODEL), f32),
        "w_ada": 0.5 * nrm(ks[2], (DEPTH, D_MODEL, N_SUBLAYERS * 3 * D_MODEL), D_MODEL),
        "b_ada": 0.01 * jax.random.normal(ks[3], (DEPTH, N_SUBLAYERS * 3 * D_MODEL), f32),
        "norm_gain": 1.0 + 0.05 * jax.random.normal(ks[4], (DEPTH, 2 * N_SUBLAYERS, D_MODEL), f32),
        "ffn1_w_in": nrm(ks[5], (DEPTH, D_MODEL, 2 * D_FF), D_MODEL),
        "ffn1_w_out": nrm(ks[6], (DEPTH, D_FF, D_MODEL), D_FF),
        "mix_w_in": nrm(ks[7], (DEPTH, D_MODEL, D_IN_PROJ), D_MODEL),
        "conv_w": nrm(ks[8], (DEPTH, CONV_WIDTH, D_RNN), CONV_WIDTH),
        "conv_b": 0.01 * jax.random.normal(ks[9], (DEPTH, D_RNN), f32),
        "lru_w_x": nrm(ks[10], (DEPTH, N_LRU_BLOCKS, LRU_BLOCK, LRU_BLOCK), LRU_BLOCK),
        "lru_b_x": 0.01 * jax.random.normal(ks[11], (DEPTH, D_RNN), f32),
        "lru_w_a": nrm(ks[12], (DEPTH, N_LRU_BLOCKS, LRU_BLOCK, LRU_BLOCK), LRU_BLOCK),
        "lru_b_a": 0.01 * jax.random.normal(ks[13], (DEPTH, D_RNN), f32),
        "lru_lambda": jnp.log(u) - jnp.log1p(-u),
        "mix_w_out": nrm(ks[14], (DEPTH, D_MIX, D_MODEL), D_MIX),
        "ffn2_w_in": nrm(ks[15], (DEPTH, D_MODEL, 2 * D_FF), D_MODEL),
        "ffn2_w_out": nrm(ks[16], (DEPTH, D_FF, D_MODEL), D_FF),
    }


def reference(x, c, w_ada, b_ada, norm_gain, ffn1_w_in, ffn1_w_out, mix_w_in, conv_w, conv_b,
              lru_w_x, lru_b_x, lru_w_a, lru_b_a, lru_lambda, mix_w_out, ffn2_w_in, ffn2_w_out):
    f32 = jnp.float32
    B = x.shape[0]
    for l in range(DEPTH):
        mod = jax.nn.silu(c.astype(f32)) @ w_ada[l].astype(f32) + b_ada[l].astype(f32)
        mod = mod.reshape(B, N_SUBLAYERS, 3, D_MODEL)
        shift, scale, gate = mod[:, :, 0], mod[:, :, 1], mod[:, :, 2]
        g = norm_gain[l].astype(f32)

        h = _pre(x, g[0], shift[:, 0], scale[:, 0])
        y = _swiglu(h, ffn1_w_in[l], ffn1_w_out[l])
        x = _post_residual(x, y, g[1], gate[:, 0], FFN_RES_WEIGHT)

        h = _pre(x, g[2], shift[:, 1], scale[:, 1])
        y = _hybrid_mixer(h, mix_w_in[l], conv_w[l], conv_b[l], lru_w_x[l], lru_b_x[l],
                          lru_w_a[l], lru_b_a[l], lru_lambda[l], mix_w_out[l])
        x = _post_residual(x, y, g[3], gate[:, 1], 1.0)

        h = _pre(x, g[4], shift[:, 2], scale[:, 2])
        y = _swiglu(h, ffn2_w_in[l], ffn2_w_out[l])
        x = _post_residual(x, y, g[5], gate[:, 2], FFN_RES_WEIGHT)
    return x
```

```python
import functools
import math

import jax
import jax.numpy as jnp
from jax import lax
from jax.experimental import pallas as pl
from jax.experimental.pallas import tpu as pltpu

F32 = jnp.float32
BF16 = jnp.bfloat16

D_MODEL = 1024
SEQ = 16384
HEAD_DIM = 64
D_ATTN = 512
D_RNN = 512
LRU_BLOCK = 64
D_IN_PROJ = 3 * D_ATTN + 2 * D_RNN
DILATIONS = (1, 4, 16)
N_BACK = 128
ATTN_BLOCK = 128
CONV_WIDTH = 4
LRU_C = 8.0
D_FF = 2816
FFN_RES_WEIGHT = 0.5
RMS_EPS = 1e-6

MIB = 1024 * 1024
ROW_TILE = 512
FF_CHUNK = 256
N_FF_CHUNKS = D_FF // FF_CHUNK
ATTN_CHUNK = 2048
LANES = 128
PAIR = 2 * HEAD_DIM
MOD_TILE = 1152


def _resident(shape):
    return pl.BlockSpec(shape, lambda *_: (0,) * len(shape),
                        pipeline_mode=pl.Buffered(1))


def _rms(v):
    return v * lax.rsqrt(jnp.mean(v * v, axis=-1, keepdims=True) + RMS_EPS)


def _pre(x, mod_ref, g_ref, sub):
    shift = mod_ref[3 * sub:3 * sub + 1, :]
    scale = mod_ref[3 * sub + 1:3 * sub + 2, :]
    return _rms(x) * g_ref[2 * sub:2 * sub + 1, :] * (1.0 + scale) + shift


def _post_residual(x, y, mod_ref, g_ref, sub, weight):
    gate = mod_ref[3 * sub + 2:3 * sub + 3, :]
    return x + weight * (1.0 + gate) * _rms(y) * g_ref[2 * sub + 1:2 * sub + 2, :]


def _mod_kernel(c_ref, w_ref, b_ref, o_ref):
    c = c_ref[...]
    s = c * jax.nn.sigmoid(c)
    o_ref[...] = jnp.sum(s * w_ref[...], axis=0, keepdims=True) + b_ref[...]


def _mod_call(c_col, w_ada, b_ada):
    n = w_ada.shape[1]
    return pl.pallas_call(
        _mod_kernel,
        out_shape=jax.ShapeDtypeStruct((1, n), F32),
        grid=(n // MOD_TILE,),
        in_specs=[
            pl.BlockSpec((D_MODEL, 1), lambda j: (0, 0)),
            pl.BlockSpec((D_MODEL, MOD_TILE), lambda j: (0, j)),
            pl.BlockSpec((1, MOD_TILE), lambda j: (0, j)),
        ],
        out_specs=pl.BlockSpec((1, MOD_TILE), lambda j: (0, j)),
        compiler_params=pltpu.CompilerParams(
            dimension_semantics=("arbitrary",), vmem_limit_bytes=24 * MIB),
        name="mod",
    )(c_col, w_ada, b_ada)


def _ffn_kernel(sub, x_ref, mod_ref, g_ref, win_ref, wout_ref, o_ref, acc_ref):
    x = x_ref[...]
    h = _pre(x, mod_ref, g_ref, sub).astype(BF16)
    for c in range(N_FF_CHUNKS):
        hgu = jnp.dot(h, win_ref[c], preferred_element_type=F32)
        hg = hgu[:, :FF_CHUNK]
        hu = hgu[:, FF_CHUNK:]
        a = (hg * jax.nn.sigmoid(hg) * hu).astype(BF16)
        part = jnp.dot(a, wout_ref[c], preferred_element_type=F32)
        if c == 0:
            acc_ref[...] = part
        else:
            acc_ref[...] += part
    o_ref[...] = _post_residual(x, acc_ref[...], mod_ref, g_ref, sub, FFN_RES_WEIGHT)


def _ffn_call(sub, x, mod9, g, win_r, wout_r):
    row = pl.BlockSpec((ROW_TILE, D_MODEL), lambda i: (i, 0))
    return pl.pallas_call(
        functools.partial(_ffn_kernel, sub),
        out_shape=jax.ShapeDtypeStruct(x.shape, F32),
        grid=(x.shape[0] // ROW_TILE,),
        in_specs=[row, _resident(mod9.shape), _resident(g.shape),
                  _resident(win_r.shape), _resident(wout_r.shape)],
        out_specs=row,
        scratch_shapes=[pltpu.VMEM((ROW_TILE, D_MODEL), F32)],
        compiler_params=pltpu.CompilerParams(
            dimension_semantics=("parallel",), vmem_limit_bytes=48 * MIB),
        name=f"ffn{sub}",
    )(x, mod9, g, win_r, wout_r)


def _inproj_kernel(x_ref, mod_ref, g_ref, w_ref,
                   q1, k1, v1, q4, k4, v4, q16, k16, v16, xr_ref, gr_ref, z_ref):
    h = _pre(x_ref[...], mod_ref, g_ref, 1).astype(BF16)

    def proj(j, width):
        return jnp.dot(h, w_ref[:, j:j + width], preferred_element_type=F32)

    outs = ((q1, q4, q16), (k1, k4, k16), (v1, v4, v16))
    for n, (o1, o4, o16) in enumerate(outs):
        z = proj(n * D_ATTN, D_ATTN)
        if n == 0:
            z = z * (HEAD_DIM ** -0.5)
        o1[...] = z.astype(BF16)
        for t in range(D_ATTN // LANES):
            cols = slice(t * LANES, (t + 1) * LANES)
            z_ref[t] = z[:, cols]
            for r in range(4):
                o4[0, r, :, cols] = z_ref[
                    t, pl.ds(r, ROW_TILE // 4, stride=4), :].astype(BF16)
            for r in range(16):
                o16[0, r, :, cols] = z_ref[
                    t, pl.ds(r, ROW_TILE // 16, stride=16), :].astype(BF16)
    xr_ref[...] = proj(3 * D_ATTN, D_RNN)
    gr_ref[...] = proj(3 * D_ATTN + D_RNN, D_RNN)


def _inproj_call(x, mod9, g, w):
    s = x.shape[0]
    nat = pl.BlockSpec((ROW_TILE, D_ATTN), lambda i: (i, 0))
    c4 = pl.BlockSpec((1, 4, ROW_TILE // 4, D_ATTN), lambda i: (i, 0, 0, 0))
    per16 = ATTN_CHUNK // ROW_TILE
    c16 = pl.BlockSpec((1, 16, ROW_TILE // 16, D_ATTN),
                       lambda i: (i // per16, 0, i % per16, 0))
    s_nat = jax.ShapeDtypeStruct((s, D_ATTN), BF16)
    s_c4 = jax.ShapeDtypeStruct((s // ROW_TILE, 4, ROW_TILE // 4, D_ATTN), BF16)
    s_c16 = jax.ShapeDtypeStruct((s // ATTN_CHUNK, 16, ATTN_BLOCK, D_ATTN), BF16)
    s_f32 = jax.ShapeDtypeStruct((s, D_RNN), F32)
    return pl.pallas_call(
        _inproj_kernel,
        out_shape=[s_nat] * 3 + [s_c4] * 3 + [s_c16] * 3 + [s_f32] * 2,
        grid=(s // ROW_TILE,),
        in_specs=[pl.BlockSpec((ROW_TILE, D_MODEL), lambda i: (i, 0)),
                  _resident(mod9.shape), _resident(g.shape), _resident(w.shape)],
        out_specs=[nat] * 3 + [c4] * 3 + [c16] * 3 + [nat] * 2,
        scratch_shapes=[pltpu.VMEM((D_ATTN // LANES, ROW_TILE, LANES), F32)],
        compiler_params=pltpu.CompilerParams(
            dimension_semantics=("parallel",), vmem_limit_bytes=40 * MIB),
        name="inproj",
    )(x, mod9, g, w)


def _attn_kernel(q1, k1, v1, k1p, v1p, q4, k4, v4, k4p, v4p,
                 q16, k16, v16, k16p, v16p, o_ref,
                 kc1, vc1, kc4, vc4, kc16, vc16, acc_ref, m_ref, l_ref):
    chunk = pl.program_id(0)
    nblk = ATTN_CHUNK // ATTN_BLOCK

    kc1[0, 0:ATTN_BLOCK] = k1p[...]
    vc1[0, 0:ATTN_BLOCK] = v1p[...]
    kc1[0, ATTN_BLOCK:] = k1[...]
    vc1[0, ATTN_BLOCK:] = v1[...]
    for r in range(4):
        kc4[r, 0:ATTN_BLOCK] = k4p[0, r]
        vc4[r, 0:ATTN_BLOCK] = v4p[0, r]
        for i in range(4):
            kc4[r, (i + 1) * ATTN_BLOCK:(i + 2) * ATTN_BLOCK] = k4[i, r]
            vc4[r, (i + 1) * ATTN_BLOCK:(i + 2) * ATTN_BLOCK] = v4[i, r]
    for r in range(16):
        kc16[r, 0:ATTN_BLOCK] = k16p[0, r]
        vc16[r, 0:ATTN_BLOCK] = v16p[0, r]
        kc16[r, ATTN_BLOCK:] = k16[0, r]
        vc16[r, ATTN_BLOCK:] = v16[0, r]

    row = lax.broadcasted_iota(jnp.int32, (ATTN_BLOCK, 2 * ATTN_BLOCK), 0)
    col = lax.broadcasted_iota(jnp.int32, (ATTN_BLOCK, 2 * ATTN_BLOCK), 1)
    band = (col >= row) & (col <= row + N_BACK)
    lane = lax.broadcasted_iota(jnp.int32, (1, PAIR), 1)
    head_mask = [(lane < HEAD_DIM).astype(BF16), (lane >= HEAD_DIM).astype(BF16)]
    first_head = lax.broadcasted_iota(jnp.int32, (ATTN_BLOCK, PAIR), 1) < HEAD_DIM

    def tile(q, kk, vv, prev_ok):
        valid = band & (col >= jnp.where(prev_ok, 0, ATTN_BLOCK))
        o = None
        stats = []
        for hm in head_mask:
            s = lax.dot_general(q * hm, kk, (((1,), (1,)), ((), ())),
                                preferred_element_type=F32)
            s = jnp.where(valid, s, -jnp.inf)
            m = jnp.max(s, axis=-1, keepdims=True)
            p = jnp.exp(s - m)
            l = jnp.sum(p, axis=-1, keepdims=True)
            pv = jnp.dot(p.astype(BF16), vv * hm, preferred_element_type=F32)
            o = pv if o is None else o + pv
            stats.append((m, l))
        m_t = jnp.where(first_head, stats[0][0], stats[1][0])
        l_t = jnp.where(first_head, stats[0][1], stats[1][1])
        return o, m_t, l_t

    def merge(rows, o, m_t, l_t, last):
        m_old = m_ref[rows, :]
        m_new = jnp.maximum(m_old, m_t)
        a = jnp.exp(m_old - m_new)
        b = jnp.exp(m_t - m_new)
        acc = acc_ref[rows, :] * a + o * b
        den = l_ref[rows, :] * a + l_t * b
        if last:
            acc_ref[rows, :] = acc / den
        else:
            acc_ref[rows, :] = acc
            m_ref[rows, :] = m_new
            l_ref[rows, :] = den

    def band1(jb, carry):
        start = pl.multiple_of(jb * ATTN_BLOCK, ATTN_BLOCK)
        q = q1[pl.ds(start, ATTN_BLOCK), :]
        kk = kc1[0, pl.ds(start, 2 * ATTN_BLOCK), :]
        vv = vc1[0, pl.ds(start, 2 * ATTN_BLOCK), :]
        o, m_t, l_t = tile(q, kk, vv, (jb > 0) | (chunk > 0))
        rows = pl.ds(start, ATTN_BLOCK)
        acc_ref[rows, :] = o
        m_ref[rows, :] = m_t
        l_ref[rows, :] = l_t
        return carry

    def band4(n, carry):
        i = n >> 2
        r = n & 3
        start = pl.multiple_of(i * ATTN_BLOCK, ATTN_BLOCK)
        q = q4[i, r]
        kk = kc4[r, pl.ds(start, 2 * ATTN_BLOCK), :]
        vv = vc4[r, pl.ds(start, 2 * ATTN_BLOCK), :]
        o, m_t, l_t = tile(q, kk, vv, (i > 0) | (chunk > 0))
        rows = pl.ds(i * (4 * ATTN_BLOCK) + r, ATTN_BLOCK, stride=4)
        merge(rows, o, m_t, l_t, last=False)
        return carry

    def band16(r, carry):
        o, m_t, l_t = tile(q16[0, r], kc16[r], vc16[r], chunk > 0)
        merge(pl.ds(r, ATTN_BLOCK, stride=16), o, m_t, l_t, last=True)
        return carry

    lax.fori_loop(0, nblk, band1, 0)
    lax.fori_loop(0, nblk, band4, 0)
    lax.fori_loop(0, nblk, band16, 0)
    o_ref[...] = acc_ref[...].astype(BF16)


def _attn_call(q1, k1, v1, q4, k4, v4, q16, k16, v16):
    s = q1.shape[0]
    n_pairs = D_ATTN // PAIR
    blocks_per_chunk = ATTN_CHUNK // ATTN_BLOCK
    sub_per_chunk = ATTN_CHUNK // (4 * ATTN_BLOCK)

    nat = pl.BlockSpec((ATTN_CHUNK, PAIR), lambda c, p: (c, p))
    nat_prev = pl.BlockSpec(
        (ATTN_BLOCK, PAIR),
        lambda c, p: (jnp.maximum(c * blocks_per_chunk - 1, 0), p))
    c4 = pl.BlockSpec((sub_per_chunk, 4, ATTN_BLOCK, PAIR), lambda c, p: (c, 0, 0, p))
    c4_prev = pl.BlockSpec(
        (1, 4, ATTN_BLOCK, PAIR),
        lambda c, p: (jnp.maximum(c * sub_per_chunk - 1, 0), 0, 0, p))
    c16 = pl.BlockSpec((1, 16, ATTN_BLOCK, PAIR), lambda c, p: (c, 0, 0, p))
    c16_prev = pl.BlockSpec((1, 16, ATTN_BLOCK, PAIR),
                            lambda c, p: (jnp.maximum(c - 1, 0), 0, 0, p))
    cat = lambda classes, blocks: pltpu.VMEM(
        (classes, (blocks + 1) * ATTN_BLOCK, PAIR), BF16)
    return pl.pallas_call(
        _attn_kernel,
        out_shape=jax.ShapeDtypeStruct((s, D_ATTN), BF16),
        grid=(s // ATTN_CHUNK, n_pairs),
        in_specs=[nat, nat, nat, nat_prev, nat_prev,
                  c4, c4, c4, c4_prev, c4_prev,
                  c16, c16, c16, c16_prev, c16_prev],
        out_specs=nat,
        scratch_shapes=[cat(1, 16), cat(1, 16), cat(4, 4), cat(4, 4),
                        cat(16, 1), cat(16, 1),
                        pltpu.VMEM((ATTN_CHUNK, PAIR), F32),
                        pltpu.VMEM((ATTN_CHUNK, PAIR), F32),
                        pltpu.VMEM((ATTN_CHUNK, PAIR), F32)],
        compiler_params=pltpu.CompilerParams(
            dimension_semantics=("parallel", "parallel"),
            vmem_limit_bytes=40 * MIB),
        name="attn",
    )(q1, k1, v1, k1, v1, q4, k4, v4, k4, v4, q16, k16, v16, k16, v16)


def _rglru_kernel(xr_ref, xp_ref, gr_ref, cw_ref, cb_ref, wx_ref, wa_ref,
                  bx_ref, ba_ref, lam_ref, o_ref,
                  xcat_ref, a_ref, b_ref, h_ref, carry_ref):
    step = pl.program_id(0)
    rows = xr_ref.shape[0]
    half = D_RNN // 2

    @pl.when(step == 0)
    def _():
        carry_ref[...] = jnp.zeros_like(carry_ref)

    xcat_ref[0:8, :] = jnp.where(step > 0, xp_ref[...], 0.0)
    xcat_ref[8:, :] = xr_ref[...]
    u = cb_ref[...]
    for k in range(CONV_WIDTH):
        u = u + cw_ref[k:k + 1, :] * xcat_ref[pl.ds(8 - (CONV_WIDTH - 1) + k, rows), :]

    ub = u.astype(BF16)

    def blockdiag(w_ref):
        lo = jnp.dot(ub[:, :half], w_ref[0], preferred_element_type=F32)
        hi = jnp.dot(ub[:, half:], w_ref[1], preferred_element_type=F32)
        return jnp.concatenate([lo, hi], axis=1)

    gate_x = jax.nn.sigmoid(blockdiag(wx_ref) + bx_ref[...])
    gate_a = jax.nn.sigmoid(blockdiag(wa_ref) + ba_ref[...])
    neg_lam = -lam_ref[...]
    softplus = jnp.maximum(neg_lam, 0.0) + jnp.log1p(jnp.exp(-jnp.abs(neg_lam)))
    log_a = -LRU_C * gate_a * softplus
    a = jnp.exp(log_a)
    a_ref[...] = a
    b_ref[...] = u * gate_x * jnp.sqrt(-jnp.tanh(log_a) * (a * a + 1.0))

    sub = lax.broadcasted_iota(jnp.int32, (8, D_RNN), 0)

    def group(gi, h_prev):
        r0 = pl.multiple_of(gi * 8, 8)
        ga = a_ref[pl.ds(r0, 8), :]
        gb = b_ref[pl.ds(r0, 8), :]
        for s in (1, 2, 4):
            a_sh = jnp.where(sub < s, 1.0, pltpu.roll(ga, s, 0))
            b_sh = jnp.where(sub < s, 0.0, pltpu.roll(gb, s, 0))
            gb = ga * b_sh + gb
            ga = ga * a_sh
        h = ga * h_prev + gb
        h_ref[pl.ds(r0, 8), :] = h
        return jnp.broadcast_to(h[7:8, :], (8, D_RNN))

    carry_ref[...] = lax.fori_loop(0, rows // 8, group, carry_ref[...], unroll=8)

    g = gr_ref[...]
    gelu = 0.5 * g * (1.0 + jnp.tanh(math.sqrt(2.0 / math.pi) * (g + 0.044715 * g * g * g)))
    o_ref[...] = (h_ref[...] * gelu).astype(BF16)


def _rglru_call(xr, gr, conv_w, conv_b, wx, wa, b_x, b_a, lam):
    s = xr.shape[0]
    row = pl.BlockSpec((ROW_TILE, D_RNN), lambda i: (i, 0))
    prev8 = pl.BlockSpec((8, D_RNN),
                         lambda i: (jnp.maximum(i * (ROW_TILE // 8) - 1, 0), 0))
    vec = _resident((1, D_RNN))
    return pl.pallas_call(
        _rglru_kernel,
        out_shape=jax.ShapeDtypeStruct((s, D_RNN), BF16),
        grid=(s // ROW_TILE,),
        in_specs=[row, prev8, row, _resident(conv_w.shape), vec,
                  _resident(wx.shape), _resident(wa.shape), vec, vec, vec],
        out_specs=row,
        scratch_shapes=[pltpu.VMEM((ROW_TILE + 8, D_RNN), F32),
                        pltpu.VMEM((ROW_TILE, D_RNN), F32),
                        pltpu.VMEM((ROW_TILE, D_RNN), F32),
                        pltpu.VMEM((ROW_TILE, D_RNN), F32),
                        pltpu.VMEM((8, D_RNN), F32)],
        compiler_params=pltpu.CompilerParams(
            dimension_semantics=("arbitrary",), vmem_limit_bytes=32 * MIB),
        name="rglru",
    )(xr, xr, gr, conv_w, conv_b, wx, wa, b_x, b_a, lam)


def _outproj_kernel(x_ref, attn_ref, rec_ref, mod_ref, g_ref, w_ref, o_ref):
    y = jnp.dot(attn_ref[...], w_ref[:D_ATTN, :], preferred_element_type=F32)
    y = y + jnp.dot(rec_ref[...], w_ref[D_ATTN:, :], preferred_element_type=F32)
    o_ref[...] = _post_residual(x_ref[...], y, mod_ref, g_ref, 1, 1.0)


def _outproj_call(x, attn, rec, mod9, g, w):
    row = pl.BlockSpec((ROW_TILE, D_MODEL), lambda i: (i, 0))
    half = pl.BlockSpec((ROW_TILE, D_ATTN), lambda i: (i, 0))
    return pl.pallas_call(
        _outproj_kernel,
        out_shape=jax.ShapeDtypeStruct(x.shape, F32),
        grid=(x.shape[0] // ROW_TILE,),
        in_specs=[row, half, half, _resident(mod9.shape), _resident(g.shape),
                  _resident(w.shape)],
        out_specs=row,
        compiler_params=pltpu.CompilerParams(
            dimension_semantics=("parallel",), vmem_limit_bytes=32 * MIB),
        name="outproj",
    )(x, attn, rec, mod9, g, w)


def _ffn_weights(w_in, w_out):
    wg = w_in[:, :D_FF].reshape(D_MODEL, N_FF_CHUNKS, FF_CHUNK)
    wu = w_in[:, D_FF:].reshape(D_MODEL, N_FF_CHUNKS, FF_CHUNK)
    win_r = jnp.concatenate([wg, wu], axis=2).transpose(1, 0, 2).astype(BF16)
    wout_r = w_out.reshape(N_FF_CHUNKS, FF_CHUNK, D_MODEL).astype(BF16)
    return win_r, wout_r


def _blockdiag_halves(w):
    per_half = (D_RNN // 2) // LRU_BLOCK
    eye = jnp.eye(per_half, dtype=w.dtype)
    wh = w.reshape(2, per_half, LRU_BLOCK, LRU_BLOCK)
    dense = jnp.einsum('hgij,gk->hgikj', wh, eye)
    return dense.reshape(2, D_RNN // 2, D_RNN // 2).astype(BF16)


def kernel(x, c, w_ada, b_ada, norm_gain, ffn1_w_in, ffn1_w_out, mix_w_in, conv_w, conv_b,
           lru_w_x, lru_b_x, lru_w_a, lru_b_a, lru_lambda, mix_w_out, ffn2_w_in, ffn2_w_out):
    batch, seq, d = x.shape
    assert (batch, seq, d) == (1, SEQ, D_MODEL) and w_ada.shape[0] == 1
    xs = x.reshape(seq, d)
    g = norm_gain[0]

    mod = _mod_call(c.reshape(d, 1), w_ada[0], b_ada)
    mod9 = mod.reshape(9, d)

    win1, wout1 = _ffn_weights(ffn1_w_in[0], ffn1_w_out[0])
    win2, wout2 = _ffn_weights(ffn2_w_in[0], ffn2_w_out[0])

    x1 = _ffn_call(0, xs, mod9, g, win1, wout1)

    q1, k1, v1, q4, k4, v4, q16, k16, v16, xr, gr = _inproj_call(
        x1, mod9, g, mix_w_in[0].astype(BF16))
    attn = _attn_call(q1, k1, v1, q4, k4, v4, q16, k16, v16)
    rec = _rglru_call(xr, gr, conv_w[0], conv_b, _blockdiag_halves(lru_w_x[0]),
                      _blockdiag_halves(lru_w_a[0]), lru_b_x, lru_b_a, lru_lambda)
    x2 = _outproj_call(x1, attn, rec, mod9, g, mix_w_out[0].astype(BF16))

    x3 = _ffn_call(2, x2, mod9, g, win2, wout2)
    return x3.reshape(batch, seq, d)
```

```python
import functools
import math

import jax
import jax.numpy as jnp
from jax import lax
from jax.experimental import pallas as pl
from jax.experimental.pallas import tpu as pltpu

F32 = jnp.float32
BF16 = jnp.bfloat16

D_MODEL = 1024
SEQ = 16384
HEAD_DIM = 64
D_ATTN = 512
D_RNN = 512
LRU_BLOCK = 64
D_IN_PROJ = 3 * D_ATTN + 2 * D_RNN
DILATIONS = (1, 4, 16)
N_BACK = 128
ATTN_BLOCK = 128
CONV_WIDTH = 4
LRU_C = 8.0
D_FF = 2816
FFN_RES_WEIGHT = 0.5
RMS_EPS = 1e-6

MIB = 1024 * 1024
ROW_TILE = 512
FF_CHUNK = 256
N_FF_CHUNKS = D_FF // FF_CHUNK
ATTN_CHUNK = 2048
LANES = 128
PAIR = 2 * HEAD_DIM
MOD_TILE = 1152
ATTN_GROUP = 4


def _resident(shape):
    return pl.BlockSpec(shape, lambda *_: (0,) * len(shape),
                        pipeline_mode=pl.Buffered(1))


def _rms(v):
    return v * lax.rsqrt(jnp.mean(v * v, axis=-1, keepdims=True) + RMS_EPS)


def _pre(x, mod_ref, g_ref, sub):
    shift = mod_ref[3 * sub:3 * sub + 1, :]
    scale = mod_ref[3 * sub + 1:3 * sub + 2, :]
    return _rms(x) * g_ref[2 * sub:2 * sub + 1, :] * (1.0 + scale) + shift


def _post_residual(x, y, mod_ref, g_ref, sub, weight):
    gate = mod_ref[3 * sub + 2:3 * sub + 3, :]
    return x + weight * (1.0 + gate) * _rms(y) * g_ref[2 * sub + 1:2 * sub + 2, :]


def _mod_kernel(c_ref, w_ref, b_ref, o_ref):
    c = c_ref[...]
    s = c * jax.nn.sigmoid(c)
    o_ref[...] = jnp.sum(s * w_ref[...], axis=0, keepdims=True) + b_ref[...]


def _mod_call(c_col, w_ada, b_ada):
    n = w_ada.shape[1]
    return pl.pallas_call(
        _mod_kernel,
        out_shape=jax.ShapeDtypeStruct((1, n), F32),
        grid=(n // MOD_TILE,),
        in_specs=[
            pl.BlockSpec((D_MODEL, 1), lambda j: (0, 0)),
            pl.BlockSpec((D_MODEL, MOD_TILE), lambda j: (0, j)),
            pl.BlockSpec((1, MOD_TILE), lambda j: (0, j)),
        ],
        out_specs=pl.BlockSpec((1, MOD_TILE), lambda j: (0, j)),
        compiler_params=pltpu.CompilerParams(
            dimension_semantics=("arbitrary",), vmem_limit_bytes=24 * MIB),
        name="mod",
    )(c_col, w_ada, b_ada)


def _ffn_kernel(sub, x_ref, mod_ref, g_ref, win_ref, wout_ref, o_ref, acc_ref):
    x = x_ref[...]
    h = _pre(x, mod_ref, g_ref, sub).astype(BF16)
    for c in range(N_FF_CHUNKS):
        hgu = jnp.dot(h, win_ref[c], preferred_element_type=F32)
        hg = hgu[:, :FF_CHUNK]
        hu = hgu[:, FF_CHUNK:]
        a = (hg * jax.nn.sigmoid(hg) * hu).astype(BF16)
        part = jnp.dot(a, wout_ref[c], preferred_element_type=F32)
        if c == 0:
            acc_ref[...] = part
        else:
            acc_ref[...] += part
    o_ref[...] = _post_residual(x, acc_ref[...], mod_ref, g_ref, sub, FFN_RES_WEIGHT)


def _ffn_call(sub, x, mod9, g, win_r, wout_r):
    row = pl.BlockSpec((ROW_TILE, D_MODEL), lambda i: (i, 0))
    return pl.pallas_call(
        functools.partial(_ffn_kernel, sub),
        out_shape=jax.ShapeDtypeStruct(x.shape, F32),
        grid=(x.shape[0] // ROW_TILE,),
        in_specs=[row, _resident(mod9.shape), _resident(g.shape),
                  _resident(win_r.shape), _resident(wout_r.shape)],
        out_specs=row,
        scratch_shapes=[pltpu.VMEM((ROW_TILE, D_MODEL), F32)],
        compiler_params=pltpu.CompilerParams(
            dimension_semantics=("parallel",), vmem_limit_bytes=48 * MIB),
        name=f"ffn{sub}",
    )(x, mod9, g, win_r, wout_r)


def _inproj_kernel(x_ref, mod_ref, g_ref, w_ref,
                   q1, k1, v1, q4, k4, v4, q16, k16, v16, xr_ref, gr_ref, z_ref):
    h = _pre(x_ref[...], mod_ref, g_ref, 1).astype(BF16)

    def proj(j, width):
        return jnp.dot(h, w_ref[:, j:j + width], preferred_element_type=F32)

    outs = ((q1, q4, q16), (k1, k4, k16), (v1, v4, v16))
    for n, (o1, o4, o16) in enumerate(outs):
        z = proj(n * D_ATTN, D_ATTN)
        if n == 0:
            z = z * (HEAD_DIM ** -0.5)
        o1[...] = z.astype(BF16)
        for t in range(D_ATTN // LANES):
            cols = slice(t * LANES, (t + 1) * LANES)
            z_ref[t] = z[:, cols]
            for r in range(4):
                o4[0, r, :, cols] = z_ref[
                    t, pl.ds(r, ROW_TILE // 4, stride=4), :].astype(BF16)
            for r in range(16):
                o16[0, r, :, cols] = z_ref[
                    t, pl.ds(r, ROW_TILE // 16, stride=16), :].astype(BF16)
    xr_ref[...] = proj(3 * D_ATTN, D_RNN)
    gr_ref[...] = proj(3 * D_ATTN + D_RNN, D_RNN)


def _inproj_call(x, mod9, g, w):
    s = x.shape[0]
    nat = pl.BlockSpec((ROW_TILE, D_ATTN), lambda i: (i, 0))
    c4 = pl.BlockSpec((1, 4, ROW_TILE // 4, D_ATTN), lambda i: (i, 0, 0, 0))
    per16 = ATTN_CHUNK // ROW_TILE
    c16 = pl.BlockSpec((1, 16, ROW_TILE // 16, D_ATTN),
                       lambda i: (i // per16, 0, i % per16, 0))
    s_nat = jax.ShapeDtypeStruct((s, D_ATTN), BF16)
    s_c4 = jax.ShapeDtypeStruct((s // ROW_TILE, 4, ROW_TILE // 4, D_ATTN), BF16)
    s_c16 = jax.ShapeDtypeStruct((s // ATTN_CHUNK, 16, ATTN_BLOCK, D_ATTN), BF16)
    s_f32 = jax.ShapeDtypeStruct((s, D_RNN), F32)
    return pl.pallas_call(
        _inproj_kernel,
        out_shape=[s_nat] * 3 + [s_c4] * 3 + [s_c16] * 3 + [s_f32] * 2,
        grid=(s // ROW_TILE,),
        in_specs=[pl.BlockSpec((ROW_TILE, D_MODEL), lambda i: (i, 0)),
                  _resident(mod9.shape), _resident(g.shape), _resident(w.shape)],
        out_specs=[nat] * 3 + [c4] * 3 + [c16] * 3 + [nat] * 2,
        scratch_shapes=[pltpu.VMEM((D_ATTN // LANES, ROW_TILE, LANES), F32)],
        compiler_params=pltpu.CompilerParams(
            dimension_semantics=("parallel",), vmem_limit_bytes=40 * MIB),
        name="inproj",
    )(x, mod9, g, w)


def _attn_kernel(q1, k1, v1, k1p, v1p, q4, k4, v4, k4p, v4p,
                 q16, k16, v16, k16p, v16p, o_ref,
                 kc1, vc1, kc4, vc4, kc16, vc16,
                 s_ref, p_ref, mt_ref, lt_ref, acc_ref, m_ref, l_ref):
    chunk = pl.program_id(0)
    nblk = ATTN_CHUNK // ATTN_BLOCK
    lane = lax.broadcasted_iota(jnp.int32, (1, PAIR), 1)
    head_mask = [(lane < HEAD_DIM).astype(BF16), (lane >= HEAD_DIM).astype(BF16)]

    def put(dst, cls, rows, src):
        for h, hm in enumerate(head_mask):
            dst[h, cls, rows] = src * hm

    prev = slice(0, ATTN_BLOCK)
    put(kc1, 0, prev, k1p[...])
    put(vc1, 0, prev, v1p[...])
    put(kc1, 0, slice(ATTN_BLOCK, None), k1[...])
    put(vc1, 0, slice(ATTN_BLOCK, None), v1[...])
    for r in range(4):
        put(kc4, r, prev, k4p[0, r])
        put(vc4, r, prev, v4p[0, r])
        for i in range(4):
            rows = slice((i + 1) * ATTN_BLOCK, (i + 2) * ATTN_BLOCK)
            put(kc4, r, rows, k4[i, r])
            put(vc4, r, rows, v4[i, r])
    for r in range(16):
        put(kc16, r, prev, k16p[0, r])
        put(vc16, r, prev, v16p[0, r])
        put(kc16, r, slice(ATTN_BLOCK, None), k16[0, r])
        put(vc16, r, slice(ATTN_BLOCK, None), v16[0, r])

    row = lax.broadcasted_iota(jnp.int32, (ATTN_BLOCK, 2 * ATTN_BLOCK), 0)
    col = lax.broadcasted_iota(jnp.int32, (ATTN_BLOCK, 2 * ATTN_BLOCK), 1)
    band = (col >= row) & (col <= row + N_BACK)
    first_head = lax.broadcasted_iota(jnp.int32, (ATTN_BLOCK, PAIR), 1) < HEAD_DIM

    def band_stages(q_of, kv_of, kc, vc, prev_ok_of, rows_of, first, last):
        def scores(t):
            q = q_of(t)
            for h in range(2):
                s_ref[t, h] = lax.dot_general(
                    q, kv_of(kc, h, t), (((1,), (1,)), ((), ())),
                    preferred_element_type=F32)

        def softmax(t):
            valid = band & (col >= jnp.where(prev_ok_of(t), 0, ATTN_BLOCK))
            stats = []
            for h in range(2):
                s = jnp.where(valid, s_ref[t, h], -jnp.inf)
                m = jnp.max(s, axis=-1, keepdims=True)
                p = jnp.exp(s - m)
                stats.append((m, jnp.sum(p, axis=-1, keepdims=True)))
                p_ref[t, h] = p.astype(BF16)
            mt_ref[t] = jnp.where(first_head, stats[0][0], stats[1][0])
            lt_ref[t] = jnp.where(first_head, stats[0][1], stats[1][1])

        def values(t):
            o = (jnp.dot(p_ref[t, 0], kv_of(vc, 0, t), preferred_element_type=F32)
                 + jnp.dot(p_ref[t, 1], kv_of(vc, 1, t), preferred_element_type=F32))
            rows = rows_of(t)
            m_t = mt_ref[t]
            l_t = lt_ref[t]
            if first:
                acc_ref[rows, :] = o
                m_ref[rows, :] = m_t
                l_ref[rows, :] = l_t
                return
            m_old = m_ref[rows, :]
            m_new = jnp.maximum(m_old, m_t)
            a = jnp.exp(m_old - m_new)
            b = jnp.exp(m_t - m_new)
            acc = acc_ref[rows, :] * a + o * b
            den = l_ref[rows, :] * a + l_t * b
            if last:
                acc_ref[rows, :] = acc / den
            else:
                acc_ref[rows, :] = acc
                m_ref[rows, :] = m_new
                l_ref[rows, :] = den

        return scores, softmax, values

    def start_of(i):
        return i * ATTN_BLOCK

    bands = [
        band_stages(lambda t: q1[pl.ds(start_of(t), ATTN_BLOCK), :],
                    lambda ref, h, t: ref[h, 0, pl.ds(start_of(t), 2 * ATTN_BLOCK), :],
                    kc1, vc1,
                    lambda t: (t > 0) | (chunk > 0),
                    lambda t: pl.ds(start_of(t), ATTN_BLOCK), first=True, last=False),
        band_stages(lambda t: q4[t >> 2, t & 3],
                    lambda ref, h, t: ref[h, t & 3,
                                          pl.ds(start_of(t >> 2), 2 * ATTN_BLOCK), :],
                    kc4, vc4,
                    lambda t: ((t >> 2) > 0) | (chunk > 0),
                    lambda t: pl.ds((t >> 2) * (4 * ATTN_BLOCK) + (t & 3),
                                    ATTN_BLOCK, stride=4),
                    first=False, last=False),
        band_stages(lambda t: q16[0, t],
                    lambda ref, h, t: ref[h, t],
                    kc16, vc16,
                    lambda t: chunk > 0,
                    lambda t: pl.ds(t, ATTN_BLOCK, stride=16), first=False, last=True),
    ]

    groups = [(b, g) for b in range(len(bands)) for g in range(nblk // ATTN_GROUP)]
    depth = 3
    for i in range(len(groups) + depth - 1):
        for lag in reversed(range(depth)):
            if 0 <= i - lag < len(groups):
                b, g = groups[i - lag]
                for j in range(ATTN_GROUP):
                    bands[b][lag](g * ATTN_GROUP + j)
    o_ref[...] = acc_ref[...].astype(BF16)


def _attn_call(q1, k1, v1, q4, k4, v4, q16, k16, v16):
    s = q1.shape[0]
    n_pairs = D_ATTN // PAIR
    blocks_per_chunk = ATTN_CHUNK // ATTN_BLOCK
    sub_per_chunk = ATTN_CHUNK // (4 * ATTN_BLOCK)

    nat = pl.BlockSpec((ATTN_CHUNK, PAIR), lambda c, p: (c, p))
    nat_prev = pl.BlockSpec(
        (ATTN_BLOCK, PAIR),
        lambda c, p: (jnp.maximum(c * blocks_per_chunk - 1, 0), p))
    c4 = pl.BlockSpec((sub_per_chunk, 4, ATTN_BLOCK, PAIR), lambda c, p: (c, 0, 0, p))
    c4_prev = pl.BlockSpec(
        (1, 4, ATTN_BLOCK, PAIR),
        lambda c, p: (jnp.maximum(c * sub_per_chunk - 1, 0), 0, 0, p))
    c16 = pl.BlockSpec((1, 16, ATTN_BLOCK, PAIR), lambda c, p: (c, 0, 0, p))
    c16_prev = pl.BlockSpec((1, 16, ATTN_BLOCK, PAIR),
                            lambda c, p: (jnp.maximum(c - 1, 0), 0, 0, p))
    cat = lambda classes, blocks: pltpu.VMEM(
        (2, classes, (blocks + 1) * ATTN_BLOCK, PAIR), BF16)
    tiles = lambda cols, dtype: pltpu.VMEM(
        (blocks_per_chunk, 2, ATTN_BLOCK, cols), dtype)
    return pl.pallas_call(
        _attn_kernel,
        out_shape=jax.ShapeDtypeStruct((s, D_ATTN), BF16),
        grid=(s // ATTN_CHUNK, n_pairs),
        in_specs=[nat, nat, nat, nat_prev, nat_prev,
                  c4, c4, c4, c4_prev, c4_prev,
                  c16, c16, c16, c16_prev, c16_prev],
        out_specs=nat,
        scratch_shapes=[cat(1, 16), cat(1, 16), cat(4, 4), cat(4, 4),
                        cat(16, 1), cat(16, 1),
                        tiles(2 * ATTN_BLOCK, F32), tiles(2 * ATTN_BLOCK, BF16),
                        pltpu.VMEM((blocks_per_chunk, ATTN_BLOCK, PAIR), F32),
                        pltpu.VMEM((blocks_per_chunk, ATTN_BLOCK, PAIR), F32),
                        pltpu.VMEM((ATTN_CHUNK, PAIR), F32),
                        pltpu.VMEM((ATTN_CHUNK, PAIR), F32),
                        pltpu.VMEM((ATTN_CHUNK, PAIR), F32)],
        compiler_params=pltpu.CompilerParams(
            dimension_semantics=("parallel", "parallel"),
            vmem_limit_bytes=40 * MIB),
        name="attn",
    )(q1, k1, v1, k1, v1, q4, k4, v4, k4, v4, q16, k16, v16, k16, v16)


def _rglru_kernel(xr_ref, xp_ref, gr_ref, cw_ref, cb_ref, wx_ref, wa_ref,
                  bx_ref, ba_ref, lam_ref, o_ref,
                  xcat_ref, a_ref, b_ref, h_ref, carry_ref):
    step = pl.program_id(0)
    rows = xr_ref.shape[0]
    half = D_RNN // 2

    @pl.when(step == 0)
    def _():
        carry_ref[...] = jnp.zeros_like(carry_ref)

    xcat_ref[0:8, :] = jnp.where(step > 0, xp_ref[...], 0.0)
    xcat_ref[8:, :] = xr_ref[...]
    u = cb_ref[...]
    for k in range(CONV_WIDTH):
        u = u + cw_ref[k:k + 1, :] * xcat_ref[pl.ds(8 - (CONV_WIDTH - 1) + k, rows), :]

    ub = u.astype(BF16)

    def blockdiag(w_ref):
        lo = jnp.dot(ub[:, :half], w_ref[0], preferred_element_type=F32)
        hi = jnp.dot(ub[:, half:], w_ref[1], preferred_element_type=F32)
        return jnp.concatenate([lo, hi], axis=1)

    gate_x = jax.nn.sigmoid(blockdiag(wx_ref) + bx_ref[...])
    gate_a = jax.nn.sigmoid(blockdiag(wa_ref) + ba_ref[...])
    neg_lam = -lam_ref[...]
    softplus = jnp.maximum(neg_lam, 0.0) + jnp.log1p(jnp.exp(-jnp.abs(neg_lam)))
    log_a = -LRU_C * gate_a * softplus
    a = jnp.exp(log_a)
    a_ref[...] = a
    b_ref[...] = u * gate_x * jnp.sqrt(-jnp.tanh(log_a) * (a * a + 1.0))

    sub = lax.broadcasted_iota(jnp.int32, (8, D_RNN), 0)

    def group(gi, h_prev):
        r0 = pl.multiple_of(gi * 8, 8)
        ga = a_ref[pl.ds(r0, 8), :]
        gb = b_ref[pl.ds(r0, 8), :]
        for s in (1, 2, 4):
            a_sh = jnp.where(sub < s, 1.0, pltpu.roll(ga, s, 0))
            b_sh = jnp.where(sub < s, 0.0, pltpu.roll(gb, s, 0))
            gb = ga * b_sh + gb
            ga = ga * a_sh
        h = ga * h_prev + gb
        h_ref[pl.ds(r0, 8), :] = h
        return jnp.broadcast_to(h[7:8, :], (8, D_RNN))

    carry_ref[...] = lax.fori_loop(0, rows // 8, group, carry_ref[...], unroll=8)

    g = gr_ref[...]
    gelu = 0.5 * g * (1.0 + jnp.tanh(math.sqrt(2.0 / math.pi) * (g + 0.044715 * g * g * g)))
    o_ref[...] = (h_ref[...] * gelu).astype(BF16)


def _rglru_call(xr, gr, conv_w, conv_b, wx, wa, b_x, b_a, lam):
    s = xr.shape[0]
    row = pl.BlockSpec((ROW_TILE, D_RNN), lambda i: (i, 0))
    prev8 = pl.BlockSpec((8, D_RNN),
                         lambda i: (jnp.maximum(i * (ROW_TILE // 8) - 1, 0), 0))
    vec = _resident((1, D_RNN))
    return pl.pallas_call(
        _rglru_kernel,
        out_shape=jax.ShapeDtypeStruct((s, D_RNN), BF16),
        grid=(s // ROW_TILE,),
        in_specs=[row, prev8, row, _resident(conv_w.shape), vec,
                  _resident(wx.shape), _resident(wa.shape), vec, vec, vec],
        out_specs=row,
        scratch_shapes=[pltpu.VMEM((ROW_TILE + 8, D_RNN), F32),
                        pltpu.VMEM((ROW_TILE, D_RNN), F32),
                        pltpu.VMEM((ROW_TILE, D_RNN), F32),
                        pltpu.VMEM((ROW_TILE, D_RNN), F32),
                        pltpu.VMEM((8, D_RNN), F32)],
        compiler_params=pltpu.CompilerParams(
            dimension_semantics=("arbitrary",), vmem_limit_bytes=32 * MIB),
        name="rglru",
    )(xr, xr, gr, conv_w, conv_b, wx, wa, b_x, b_a, lam)


def _outproj_kernel(x_ref, attn_ref, rec_ref, mod_ref, g_ref, w_ref, o_ref):
    y = jnp.dot(attn_ref[...], w_ref[:D_ATTN, :], preferred_element_type=F32)
    y = y + jnp.dot(rec_ref[...], w_ref[D_ATTN:, :], preferred_element_type=F32)
    o_ref[...] = _post_residual(x_ref[...], y, mod_ref, g_ref, 1, 1.0)


def _outproj_call(x, attn, rec, mod9, g, w):
    row = pl.BlockSpec((ROW_TILE, D_MODEL), lambda i: (i, 0))
    half = pl.BlockSpec((ROW_TILE, D_ATTN), lambda i: (i, 0))
    return pl.pallas_call(
        _outproj_kernel,
        out_shape=jax.ShapeDtypeStruct(x.shape, F32),
        grid=(x.shape[0] // ROW_TILE,),
        in_specs=[row, half, half, _resident(mod9.shape), _resident(g.shape),
                  _resident(w.shape)],
        out_specs=row,
        compiler_params=pltpu.CompilerParams(
            dimension_semantics=("parallel",), vmem_limit_bytes=32 * MIB),
        name="outproj",
    )(x, attn, rec, mod9, g, w)


def _ffn_weights(w_in, w_out):
    wg = w_in[:, :D_FF].reshape(D_MODEL, N_FF_CHUNKS, FF_CHUNK)
    wu = w_in[:, D_FF:].reshape(D_MODEL, N_FF_CHUNKS, FF_CHUNK)
    win_r = jnp.concatenate([wg, wu], axis=2).transpose(1, 0, 2).astype(BF16)
    wout_r = w_out.reshape(N_FF_CHUNKS, FF_CHUNK, D_MODEL).astype(BF16)
    return win_r, wout_r


def _blockdiag_halves(w):
    per_half = (D_RNN // 2) // LRU_BLOCK
    eye = jnp.eye(per_half, dtype=w.dtype)
    wh = w.reshape(2, per_half, LRU_BLOCK, LRU_BLOCK)
    dense = jnp.einsum('hgij,gk->hgikj', wh, eye)
    return dense.reshape(2, D_RNN // 2, D_RNN // 2).astype(BF16)


def kernel(x, c, w_ada, b_ada, norm_gain, ffn1_w_in, ffn1_w_out, mix_w_in, conv_w, conv_b,
           lru_w_x, lru_b_x, lru_w_a, lru_b_a, lru_lambda, mix_w_out, ffn2_w_in, ffn2_w_out):
    batch, seq, d = x.shape
    assert (batch, seq, d) == (1, SEQ, D_MODEL) and w_ada.shape[0] == 1
    xs = x.reshape(seq, d)
    g = norm_gain[0]

    mod = _mod_call(c.reshape(d, 1), w_ada[0], b_ada)
    mod9 = mod.reshape(9, d)

    win1, wout1 = _ffn_weights(ffn1_w_in[0], ffn1_w_out[0])
    win2, wout2 = _ffn_weights(ffn2_w_in[0], ffn2_w_out[0])

    x1 = _ffn_call(0, xs, mod9, g, win1, wout1)

    q1, k1, v1, q4, k4, v4, q16, k16, v16, xr, gr = _inproj_call(
        x1, mod9, g, mix_w_in[0].astype(BF16))
    attn = _attn_call(q1, k1, v1, q4, k4, v4, q16, k16, v16)
    rec = _rglru_call(xr, gr, conv_w[0], conv_b, _blockdiag_halves(lru_w_x[0]),
                      _blockdiag_halves(lru_w_a[0]), lru_b_x, lru_b_a, lru_lambda)
    x2 = _outproj_call(x1, attn, rec, mod9, g, mix_w_out[0].astype(BF16))

    x3 = _ffn_call(2, x2, mod9, g, win2, wout2)
    return x3.reshape(batch, seq, d)
```

```python
import functools
import math

import jax
import jax.numpy as jnp
from jax import lax
from jax.experimental import pallas as pl
from jax.experimental.pallas import tpu as pltpu

F32 = jnp.float32
BF16 = jnp.bfloat16

D_MODEL = 1024
SEQ = 16384
HEAD_DIM = 64
D_ATTN = 512
D_RNN = 512
LRU_BLOCK = 64
D_IN_PROJ = 3 * D_ATTN + 2 * D_RNN
DILATIONS = (1, 4, 16)
N_BACK = 128
ATTN_BLOCK = 128
CONV_WIDTH = 4
LRU_C = 8.0
D_FF = 2816
FFN_RES_WEIGHT = 0.5
RMS_EPS = 1e-6

MIB = 1024 * 1024
ROW_TILE = 512
FF_CHUNK = 256
N_FF_CHUNKS = D_FF // FF_CHUNK
ATTN_CHUNK = 2048
LANES = 128
PAIR = 2 * HEAD_DIM
MOD_TILE = 1152
ATTN_GROUP = 4


def _resident(shape):
    return pl.BlockSpec(shape, lambda *_: (0,) * len(shape),
                        pipeline_mode=pl.Buffered(1))


def _rms(v):
    return v * lax.rsqrt(jnp.mean(v * v, axis=-1, keepdims=True) + RMS_EPS)


def _pre(x, mod_ref, g_ref, sub):
    shift = mod_ref[3 * sub:3 * sub + 1, :]
    scale = mod_ref[3 * sub + 1:3 * sub + 2, :]
    return _rms(x) * g_ref[2 * sub:2 * sub + 1, :] * (1.0 + scale) + shift


def _post_residual(x, y, mod_ref, g_ref, sub, weight):
    gate = mod_ref[3 * sub + 2:3 * sub + 3, :]
    return x + weight * (1.0 + gate) * _rms(y) * g_ref[2 * sub + 1:2 * sub + 2, :]


def _mod_kernel(c_ref, w_ref, b_ref, o_ref):
    c = c_ref[...]
    s = c * jax.nn.sigmoid(c)
    o_ref[...] = jnp.sum(s * w_ref[...], axis=0, keepdims=True) + b_ref[...]


def _mod_call(c_col, w_ada, b_ada):
    n = w_ada.shape[1]
    return pl.pallas_call(
        _mod_kernel,
        out_shape=jax.ShapeDtypeStruct((1, n), F32),
        grid=(n // MOD_TILE,),
        in_specs=[
            pl.BlockSpec((D_MODEL, 1), lambda j: (0, 0)),
            pl.BlockSpec((D_MODEL, MOD_TILE), lambda j: (0, j)),
            pl.BlockSpec((1, MOD_TILE), lambda j: (0, j)),
        ],
        out_specs=pl.BlockSpec((1, MOD_TILE), lambda j: (0, j)),
        compiler_params=pltpu.CompilerParams(
            dimension_semantics=("arbitrary",), vmem_limit_bytes=24 * MIB),
        name="mod",
    )(c_col, w_ada, b_ada)


def _ffn(sub, x, mod_ref, g_ref, win_ref, wout_ref, acc_ref):
    h = _pre(x, mod_ref, g_ref, sub).astype(BF16)
    for c in range(N_FF_CHUNKS):
        cols = slice(c * FF_CHUNK, (c + 1) * FF_CHUNK)
        up_cols = slice(D_FF + c * FF_CHUNK, D_FF + (c + 1) * FF_CHUNK)
        hg = jnp.dot(h, win_ref[:, cols], preferred_element_type=F32)
        hu = jnp.dot(h, win_ref[:, up_cols], preferred_element_type=F32)
        a = (hg * jax.nn.sigmoid(hg) * hu).astype(BF16)
        part = jnp.dot(a, wout_ref[cols, :], preferred_element_type=F32)
        if c == 0:
            acc_ref[...] = part
        else:
            acc_ref[...] += part
    return _post_residual(x, acc_ref[...], mod_ref, g_ref, sub, FFN_RES_WEIGHT)


def _ffn_inproj_kernel(x_ref, mod_ref, g_ref, win_ref, wout_ref, wmix_ref,
                       x1_ref, q1, k1, v1, q4, k4, v4, q16, k16, v16, xr_ref, gr_ref,
                       acc_ref, z_ref, z4_ref):
    x1 = _ffn(0, x_ref[...], mod_ref, g_ref, win_ref, wout_ref, acc_ref)
    x1_ref[...] = x1
    h = _pre(x1, mod_ref, g_ref, 1).astype(BF16)

    def proj(j, width):
        return jnp.dot(h, wmix_ref[:, j:j + width], preferred_element_type=F32)

    outs = ((q1, q4, q16), (k1, k4, k16), (v1, v4, v16))
    for n, (o1, o4, o16) in enumerate(outs):
        z = proj(n * D_ATTN, D_ATTN)
        if n == 0:
            z = z * (HEAD_DIM ** -0.5)
        o1[...] = z.astype(BF16)
        for t in range(D_ATTN // LANES):
            cols = slice(t * LANES, (t + 1) * LANES)
            z_ref[n, t] = z[:, cols]
            for r4 in range(4):
                z4 = z_ref[n, t, pl.ds(r4, ROW_TILE // 4, stride=4), :]
                o4[0, r4, :, cols] = z4.astype(BF16)
                z4_ref[n, t, r4] = z4
                for e in range(4):
                    o16[0, r4 + 4 * e, :, cols] = z4_ref[
                        n, t, r4, pl.ds(e, ROW_TILE // 16, stride=4), :].astype(BF16)
    xr_ref[...] = proj(3 * D_ATTN, D_RNN)
    gr_ref[...] = proj(3 * D_ATTN + D_RNN, D_RNN)


def _ffn_inproj_call(x, mod9, g, w_in, w_out, w_mix):
    s = x.shape[0]
    row = pl.BlockSpec((ROW_TILE, D_MODEL), lambda i: (i, 0))
    nat = pl.BlockSpec((ROW_TILE, D_ATTN), lambda i: (i, 0))
    c4 = pl.BlockSpec((1, 4, ROW_TILE // 4, D_ATTN), lambda i: (i, 0, 0, 0))
    per16 = ATTN_CHUNK // ROW_TILE
    c16 = pl.BlockSpec((1, 16, ROW_TILE // 16, D_ATTN),
                       lambda i: (i // per16, 0, i % per16, 0))
    s_nat = jax.ShapeDtypeStruct((s, D_ATTN), BF16)
    s_c4 = jax.ShapeDtypeStruct((s // ROW_TILE, 4, ROW_TILE // 4, D_ATTN), BF16)
    s_c16 = jax.ShapeDtypeStruct((s // ATTN_CHUNK, 16, ATTN_BLOCK, D_ATTN), BF16)
    s_f32 = jax.ShapeDtypeStruct((s, D_RNN), F32)
    return pl.pallas_call(
        _ffn_inproj_kernel,
        out_shape=([jax.ShapeDtypeStruct(x.shape, F32)]
                   + [s_nat] * 3 + [s_c4] * 3 + [s_c16] * 3 + [s_f32] * 2),
        grid=(s // ROW_TILE,),
        in_specs=[row, _resident(mod9.shape), _resident(g.shape),
                  _resident(w_in.shape), _resident(w_out.shape),
                  _resident(w_mix.shape)],
        out_specs=[row] + [nat] * 3 + [c4] * 3 + [c16] * 3 + [nat] * 2,
        scratch_shapes=[pltpu.VMEM((ROW_TILE, D_MODEL), F32),
                        pltpu.VMEM((3, D_ATTN // LANES, ROW_TILE, LANES), F32),
                        pltpu.VMEM((3, D_ATTN // LANES, 4, ROW_TILE // 4, LANES), F32)],
        compiler_params=pltpu.CompilerParams(
            dimension_semantics=("parallel",), vmem_limit_bytes=58 * MIB),
        name="ffn_inproj",
    )(x, mod9, g, w_in, w_out, w_mix)


def _attn_kernel(q1, k1, v1, k1p, v1p, q4, k4, v4, k4p, v4p,
                 q16, k16, v16, k16p, v16p, o_ref,
                 kc1, vc1, kc4, vc4, kc16, vc16,
                 s_ref, p_ref, mt_ref, lt_ref, acc_ref, m_ref, l_ref):
    chunk = pl.program_id(0)
    nblk = ATTN_CHUNK // ATTN_BLOCK
    lane = lax.broadcasted_iota(jnp.int32, (1, PAIR), 1)
    head_mask = [(lane < HEAD_DIM).astype(BF16), (lane >= HEAD_DIM).astype(BF16)]

    def put(dst, cls, rows, src):
        for h, hm in enumerate(head_mask):
            dst[h, cls, rows] = src * hm

    prev = slice(0, ATTN_BLOCK)
    put(kc1, 0, prev, k1p[...])
    put(vc1, 0, prev, v1p[...])
    put(kc1, 0, slice(ATTN_BLOCK, None), k1[...])
    put(vc1, 0, slice(ATTN_BLOCK, None), v1[...])
    for r in range(4):
        put(kc4, r, prev, k4p[0, r])
        put(vc4, r, prev, v4p[0, r])
        for i in range(4):
            rows = slice((i + 1) * ATTN_BLOCK, (i + 2) * ATTN_BLOCK)
            put(kc4, r, rows, k4[i, r])
            put(vc4, r, rows, v4[i, r])
    for r in range(16):
        put(kc16, r, prev, k16p[0, r])
        put(vc16, r, prev, v16p[0, r])
        put(kc16, r, slice(ATTN_BLOCK, None), k16[0, r])
        put(vc16, r, slice(ATTN_BLOCK, None), v16[0, r])

    row = lax.broadcasted_iota(jnp.int32, (ATTN_BLOCK, 2 * ATTN_BLOCK), 0)
    col = lax.broadcasted_iota(jnp.int32, (ATTN_BLOCK, 2 * ATTN_BLOCK), 1)
    band = (col >= row) & (col <= row + N_BACK)
    first_head = lax.broadcasted_iota(jnp.int32, (ATTN_BLOCK, PAIR), 1) < HEAD_DIM

    def band_stages(q_of, kv_of, kc, vc, prev_ok_of, rows_of, first, last):
        def scores(t):
            q = q_of(t)
            for h in range(2):
                s_ref[t, h] = lax.dot_general(
                    q, kv_of(kc, h, t), (((1,), (1,)), ((), ())),
                    preferred_element_type=F32)

        def softmax(t):
            valid = band & (col >= jnp.where(prev_ok_of(t), 0, ATTN_BLOCK))
            stats = []
            for h in range(2):
                s = jnp.where(valid, s_ref[t, h], -jnp.inf)
                m = jnp.max(s, axis=-1, keepdims=True)
                p = jnp.exp(s - m)
                stats.append((m, jnp.sum(p, axis=-1, keepdims=True)))
                p_ref[t, h] = p.astype(BF16)
            mt_ref[t] = jnp.where(first_head, stats[0][0], stats[1][0])
            lt_ref[t] = jnp.where(first_head, stats[0][1], stats[1][1])

        def values(t):
            o = (jnp.dot(p_ref[t, 0], kv_of(vc, 0, t), preferred_element_type=F32)
                 + jnp.dot(p_ref[t, 1], kv_of(vc, 1, t), preferred_element_type=F32))
            rows = rows_of(t)
            m_t = mt_ref[t]
            l_t = lt_ref[t]
            if first:
                acc_ref[rows, :] = o
                m_ref[rows, :] = m_t
                l_ref[rows, :] = l_t
                return
            m_old = m_ref[rows, :]
            m_new = jnp.maximum(m_old, m_t)
            a = jnp.exp(m_old - m_new)
            b = jnp.exp(m_t - m_new)
            acc = acc_ref[rows, :] * a + o * b
            den = l_ref[rows, :] * a + l_t * b
            if last:
                acc_ref[rows, :] = acc / den
            else:
                acc_ref[rows, :] = acc
                m_ref[rows, :] = m_new
                l_ref[rows, :] = den

        return scores, softmax, values

    def start_of(i):
        return i * ATTN_BLOCK

    bands = [
        band_stages(lambda t: q1[pl.ds(start_of(t), ATTN_BLOCK), :],
                    lambda ref, h, t: ref[h, 0, pl.ds(start_of(t), 2 * ATTN_BLOCK), :],
                    kc1, vc1,
                    lambda t: (t > 0) | (chunk > 0),
                    lambda t: pl.ds(start_of(t), ATTN_BLOCK), first=True, last=False),
        band_stages(lambda t: q4[t >> 2, t & 3],
                    lambda ref, h, t: ref[h, t & 3,
                                          pl.ds(start_of(t >> 2), 2 * ATTN_BLOCK), :],
                    kc4, vc4,
                    lambda t: ((t >> 2) > 0) | (chunk > 0),
                    lambda t: pl.ds((t >> 2) * (4 * ATTN_BLOCK) + (t & 3),
                                    ATTN_BLOCK, stride=4),
                    first=False, last=False),
        band_stages(lambda t: q16[0, t],
                    lambda ref, h, t: ref[h, t],
                    kc16, vc16,
                    lambda t: chunk > 0,
                    lambda t: pl.ds(t, ATTN_BLOCK, stride=16), first=False, last=True),
    ]

    groups = [(b, g) for b in range(len(bands)) for g in range(nblk // ATTN_GROUP)]
    depth = 3
    for i in range(len(groups) + depth - 1):
        for lag in reversed(range(depth)):
            if 0 <= i - lag < len(groups):
                b, g = groups[i - lag]
                for j in range(ATTN_GROUP):
                    bands[b][lag](g * ATTN_GROUP + j)
    o_ref[...] = acc_ref[...].astype(BF16)


def _attn_call(q1, k1, v1, q4, k4, v4, q16, k16, v16):
    s = q1.shape[0]
    n_pairs = D_ATTN // PAIR
    blocks_per_chunk = ATTN_CHUNK // ATTN_BLOCK
    sub_per_chunk = ATTN_CHUNK // (4 * ATTN_BLOCK)

    nat = pl.BlockSpec((ATTN_CHUNK, PAIR), lambda c, p: (c, p))
    nat_prev = pl.BlockSpec(
        (ATTN_BLOCK, PAIR),
        lambda c, p: (jnp.maximum(c * blocks_per_chunk - 1, 0), p))
    c4 = pl.BlockSpec((sub_per_chunk, 4, ATTN_BLOCK, PAIR), lambda c, p: (c, 0, 0, p))
    c4_prev = pl.BlockSpec(
        (1, 4, ATTN_BLOCK, PAIR),
        lambda c, p: (jnp.maximum(c * sub_per_chunk - 1, 0), 0, 0, p))
    c16 = pl.BlockSpec((1, 16, ATTN_BLOCK, PAIR), lambda c, p: (c, 0, 0, p))
    c16_prev = pl.BlockSpec((1, 16, ATTN_BLOCK, PAIR),
                            lambda c, p: (jnp.maximum(c - 1, 0), 0, 0, p))
    cat = lambda classes, blocks: pltpu.VMEM(
        (2, classes, (blocks + 1) * ATTN_BLOCK, PAIR), BF16)
    tiles = lambda cols, dtype: pltpu.VMEM(
        (blocks_per_chunk, 2, ATTN_BLOCK, cols), dtype)
    return pl.pallas_call(
        _attn_kernel,
        out_shape=jax.ShapeDtypeStruct((s, D_ATTN), BF16),
        grid=(s // ATTN_CHUNK, n_pairs),
        in_specs=[nat, nat, nat, nat_prev, nat_prev,
                  c4, c4, c4, c4_prev, c4_prev,
                  c16, c16, c16, c16_prev, c16_prev],
        out_specs=nat,
        scratch_shapes=[cat(1, 16), cat(1, 16), cat(4, 4), cat(4, 4),
                        cat(16, 1), cat(16, 1),
                        tiles(2 * ATTN_BLOCK, F32), tiles(2 * ATTN_BLOCK, BF16),
                        pltpu.VMEM((blocks_per_chunk, ATTN_BLOCK, PAIR), F32),
                        pltpu.VMEM((blocks_per_chunk, ATTN_BLOCK, PAIR), F32),
                        pltpu.VMEM((ATTN_CHUNK, PAIR), F32),
                        pltpu.VMEM((ATTN_CHUNK, PAIR), F32),
                        pltpu.VMEM((ATTN_CHUNK, PAIR), F32)],
        compiler_params=pltpu.CompilerParams(
            dimension_semantics=("parallel", "parallel"),
            vmem_limit_bytes=40 * MIB),
        name="attn",
    )(q1, k1, v1, k1, v1, q4, k4, v4, k4, v4, q16, k16, v16, k16, v16)


def _rglru_kernel(xr_ref, xp_ref, gr_ref, cw_ref, cb_ref, wx_ref, wa_ref,
                  bx_ref, ba_ref, lam_ref, o_ref,
                  xcat_ref, a_ref, b_ref, h_ref, carry_ref):
    step = pl.program_id(0)
    rows = xr_ref.shape[0]
    half = D_RNN // 2

    @pl.when(step == 0)
    def _():
        carry_ref[...] = jnp.zeros_like(carry_ref)

    xcat_ref[0:8, :] = jnp.where(step > 0, xp_ref[...], 0.0)
    xcat_ref[8:, :] = xr_ref[...]
    u = cb_ref[...]
    for k in range(CONV_WIDTH):
        u = u + cw_ref[k:k + 1, :] * xcat_ref[pl.ds(8 - (CONV_WIDTH - 1) + k, rows), :]

    ub = u.astype(BF16)

    def blockdiag(w_ref):
        lo = jnp.dot(ub[:, :half], w_ref[0], preferred_element_type=F32)
        hi = jnp.dot(ub[:, half:], w_ref[1], preferred_element_type=F32)
        return jnp.concatenate([lo, hi], axis=1)

    gate_x = jax.nn.sigmoid(blockdiag(wx_ref) + bx_ref[...])
    gate_a = jax.nn.sigmoid(blockdiag(wa_ref) + ba_ref[...])
    neg_lam = -lam_ref[...]
    softplus = jnp.maximum(neg_lam, 0.0) + jnp.log1p(jnp.exp(-jnp.abs(neg_lam)))
    log_a = -LRU_C * gate_a * softplus
    a = jnp.exp(log_a)
    a_ref[...] = a
    b_ref[...] = u * gate_x * jnp.sqrt(-jnp.tanh(log_a) * (a * a + 1.0))

    sub = lax.broadcasted_iota(jnp.int32, (8, D_RNN), 0)

    def group(gi, h_prev):
        r0 = pl.multiple_of(gi * 8, 8)
        ga = a_ref[pl.ds(r0, 8), :]
        gb = b_ref[pl.ds(r0, 8), :]
        for s in (1, 2, 4):
            a_sh = jnp.where(sub < s, 1.0, pltpu.roll(ga, s, 0))
            b_sh = jnp.where(sub < s, 0.0, pltpu.roll(gb, s, 0))
            gb = ga * b_sh + gb
            ga = ga * a_sh
        h = ga * h_prev + gb
        h_ref[pl.ds(r0, 8), :] = h
        return jnp.broadcast_to(h[7:8, :], (8, D_RNN))

    carry_ref[...] = lax.fori_loop(0, rows // 8, group, carry_ref[...], unroll=8)

    g = gr_ref[...]
    gelu = 0.5 * g * (1.0 + jnp.tanh(math.sqrt(2.0 / math.pi) * (g + 0.044715 * g * g * g)))
    o_ref[...] = (h_ref[...] * gelu).astype(BF16)


def _rglru_call(xr, gr, conv_w, conv_b, wx, wa, b_x, b_a, lam):
    s = xr.shape[0]
    row = pl.BlockSpec((ROW_TILE, D_RNN), lambda i: (i, 0))
    prev8 = pl.BlockSpec((8, D_RNN),
                         lambda i: (jnp.maximum(i * (ROW_TILE // 8) - 1, 0), 0))
    vec = _resident((1, D_RNN))
    return pl.pallas_call(
        _rglru_kernel,
        out_shape=jax.ShapeDtypeStruct((s, D_RNN), BF16),
        grid=(s // ROW_TILE,),
        in_specs=[row, prev8, row, _resident(conv_w.shape), vec,
                  _resident(wx.shape), _resident(wa.shape), vec, vec, vec],
        out_specs=row,
        scratch_shapes=[pltpu.VMEM((ROW_TILE + 8, D_RNN), F32),
                        pltpu.VMEM((ROW_TILE, D_RNN), F32),
                        pltpu.VMEM((ROW_TILE, D_RNN), F32),
                        pltpu.VMEM((ROW_TILE, D_RNN), F32),
                        pltpu.VMEM((8, D_RNN), F32)],
        compiler_params=pltpu.CompilerParams(
            dimension_semantics=("arbitrary",), vmem_limit_bytes=32 * MIB),
        name="rglru",
    )(xr, xr, gr, conv_w, conv_b, wx, wa, b_x, b_a, lam)


def _outproj_ffn_kernel(x_ref, attn_ref, rec_ref, mod_ref, g_ref, wmix_ref,
                        win_ref, wout_ref, o_ref, acc_ref):
    y = jnp.dot(attn_ref[...], wmix_ref[:D_ATTN, :], preferred_element_type=F32)
    y = y + jnp.dot(rec_ref[...], wmix_ref[D_ATTN:, :], preferred_element_type=F32)
    x2 = _post_residual(x_ref[...], y, mod_ref, g_ref, 1, 1.0)
    o_ref[...] = _ffn(2, x2, mod_ref, g_ref, win_ref, wout_ref, acc_ref)


def _outproj_ffn_call(x, attn, rec, mod9, g, w_mix, w_in, w_out):
    row = pl.BlockSpec((ROW_TILE, D_MODEL), lambda i: (i, 0))
    half = pl.BlockSpec((ROW_TILE, D_ATTN), lambda i: (i, 0))
    return pl.pallas_call(
        _outproj_ffn_kernel,
        out_shape=jax.ShapeDtypeStruct(x.shape, F32),
        grid=(x.shape[0] // ROW_TILE,),
        in_specs=[row, half, half, _resident(mod9.shape), _resident(g.shape),
                  _resident(w_mix.shape), _resident(w_in.shape),
                  _resident(w_out.shape)],
        out_specs=row,
        scratch_shapes=[pltpu.VMEM((ROW_TILE, D_MODEL), F32)],
        compiler_params=pltpu.CompilerParams(
            dimension_semantics=("parallel",), vmem_limit_bytes=52 * MIB),
        name="outproj_ffn",
    )(x, attn, rec, mod9, g, w_mix, w_in, w_out)


def _blockdiag_halves(w):
    per_half = (D_RNN // 2) // LRU_BLOCK
    eye = jnp.eye(per_half, dtype=w.dtype)
    wh = w.reshape(2, per_half, LRU_BLOCK, LRU_BLOCK)
    dense = jnp.einsum('hgij,gk->hgikj', wh, eye)
    return dense.reshape(2, D_RNN // 2, D_RNN // 2).astype(BF16)


def kernel(x, c, w_ada, b_ada, norm_gain, ffn1_w_in, ffn1_w_out, mix_w_in, conv_w, conv_b,
           lru_w_x, lru_b_x, lru_w_a, lru_b_a, lru_lambda, mix_w_out, ffn2_w_in, ffn2_w_out):
    batch, seq, d = x.shape
    assert (batch, seq, d) == (1, SEQ, D_MODEL) and w_ada.shape[0] == 1
    xs = x.reshape(seq, d)
    g = norm_gain[0]

    mod = _mod_call(c.reshape(d, 1), w_ada[0], b_ada)
    mod9 = mod.reshape(9, d)

    x1, q1, k1, v1, q4, k4, v4, q16, k16, v16, xr, gr = _ffn_inproj_call(
        xs, mod9, g, ffn1_w_in[0].astype(BF16), ffn1_w_out[0].astype(BF16),
        mix_w_in[0].astype(BF16))
    attn = _attn_call(q1, k1, v1, q4, k4, v4, q16, k16, v16)
    rec = _rglru_call(xr, gr, conv_w[0], conv_b, _blockdiag_halves(lru_w_x[0]),
                      _blockdiag_halves(lru_w_a[0]), lru_b_x, lru_b_a, lru_lambda)
    x3 = _outproj_ffn_call(x1, attn, rec, mod9, g, mix_w_out[0].astype(BF16),
                           ffn2_w_in[0].astype(BF16), ffn2_w_out[0].astype(BF16))
    return x3.reshape(batch, seq, d)
```

```python
import math

import jax
import jax.numpy as jnp
from jax import lax
from jax.experimental import pallas as pl
from jax.experimental.pallas import tpu as pltpu

F32 = jnp.float32
BF16 = jnp.bfloat16

D_MODEL = 1024
SEQ = 16384
HEAD_DIM = 64
D_ATTN = 512
D_RNN = 512
LRU_BLOCK = 64
N_BACK = 128
ATTN_BLOCK = 128
CONV_WIDTH = 4
LRU_C = 8.0
D_FF = 2816
FFN_RES_WEIGHT = 0.5
RMS_EPS = 1e-6

MIB = 1024 * 1024
ROW_TILE = 512
FF_CHUNK = 256
N_FF_CHUNKS = D_FF // FF_CHUNK
ATTN_CHUNK = 2048
LANES = 128
PAIR = 2 * HEAD_DIM
MOD_TILE = 1152
ATTN_GROUP = 4


def _resident(shape):
    return pl.BlockSpec(shape, lambda *_: (0,) * len(shape),
                        pipeline_mode=pl.Buffered(1))


def _rms(v):
    return v * lax.rsqrt(jnp.mean(v * v, axis=-1, keepdims=True) + RMS_EPS)


def _pre(x, mod_ref, g_ref, sub):
    shift = mod_ref[3 * sub:3 * sub + 1, :]
    scale = mod_ref[3 * sub + 1:3 * sub + 2, :]
    return _rms(x) * g_ref[2 * sub:2 * sub + 1, :] * (1.0 + scale) + shift


def _post_residual(x, y, mod_ref, g_ref, sub, weight):
    gate = mod_ref[3 * sub + 2:3 * sub + 3, :]
    return x + weight * (1.0 + gate) * _rms(y) * g_ref[2 * sub + 1:2 * sub + 2, :]


def _mod_kernel(c_ref, w_ref, b_ref, o_ref):
    c = c_ref[...]
    s = c * jax.nn.sigmoid(c)
    o_ref[...] = jnp.sum(s * w_ref[...], axis=0, keepdims=True) + b_ref[...]


def _mod_call(c_col, w_ada, b_ada):
    n = w_ada.shape[1]
    return pl.pallas_call(
        _mod_kernel,
        out_shape=jax.ShapeDtypeStruct((1, n), F32),
        grid=(n // MOD_TILE,),
        in_specs=[
            pl.BlockSpec((D_MODEL, 1), lambda j: (0, 0)),
            pl.BlockSpec((D_MODEL, MOD_TILE), lambda j: (0, j)),
            pl.BlockSpec((1, MOD_TILE), lambda j: (0, j)),
        ],
        out_specs=pl.BlockSpec((1, MOD_TILE), lambda j: (0, j)),
        compiler_params=pltpu.CompilerParams(
            dimension_semantics=("arbitrary",), vmem_limit_bytes=24 * MIB),
        name="mod",
    )(c_col, w_ada, b_ada)


def _ffn(sub, x, mod_ref, g_ref, win_ref, wout_ref, acc_ref):
    h = _pre(x, mod_ref, g_ref, sub).astype(BF16)
    for c in range(N_FF_CHUNKS):
        cols = slice(c * FF_CHUNK, (c + 1) * FF_CHUNK)
        up_cols = slice(D_FF + c * FF_CHUNK, D_FF + (c + 1) * FF_CHUNK)
        hg = jnp.dot(h, win_ref[:, cols], preferred_element_type=F32)
        hu = jnp.dot(h, win_ref[:, up_cols], preferred_element_type=F32)
        a = (hg * jax.nn.sigmoid(hg) * hu).astype(BF16)
        part = jnp.dot(a, wout_ref[cols, :], preferred_element_type=F32)
        if c == 0:
            acc_ref[...] = part
        else:
            acc_ref[...] += part
    return _post_residual(x, acc_ref[...], mod_ref, g_ref, sub, FFN_RES_WEIGHT)


def _ffn_inproj_kernel(x_ref, mod_ref, g_ref, win_ref, wout_ref, wmix_ref,
                       x1_ref, k1, va1, vb1, q4, k4, va4, vb4,
                       q16, k16, va16, vb16, xr_ref, gr_ref,
                       acc_ref, z_ref, z4_ref):
    x1 = _ffn(0, x_ref[...], mod_ref, g_ref, win_ref, wout_ref, acc_ref)
    x1_ref[...] = x1
    h = _pre(x1, mod_ref, g_ref, 1).astype(BF16)

    def proj(j, width):
        return jnp.dot(h, wmix_ref[:, j:j + width], preferred_element_type=F32)

    first_head = lax.broadcasted_iota(jnp.int32, (1, PAIR), 1) < HEAD_DIM

    def emit(n, dsts, idx, tile):
        if n < 2:
            dsts[0][idx] = tile.astype(BF16)
        else:
            dsts[0][idx] = jnp.where(first_head, tile, 0.0).astype(BF16)
            dsts[1][idx] = jnp.where(first_head, 0.0, tile).astype(BF16)

    outs = ((None, (q4,), (q16,)), ((k1,), (k4,), (k16,)),
            ((va1, vb1), (va4, vb4), (va16, vb16)))
    for n, (o1, o4, o16) in enumerate(outs):
        z = proj(n * D_ATTN, D_ATTN)
        if n == 0:
            z = z * (HEAD_DIM ** -0.5 * math.log2(math.e))
        for t in range(D_ATTN // LANES):
            cols = slice(t * LANES, (t + 1) * LANES)
            if o1 is not None:
                emit(n, o1, (slice(None), cols), z[:, cols])
            z_ref[n, t] = z[:, cols]
            for r4 in range(4):
                z4 = z_ref[n, t, pl.ds(r4, ROW_TILE // 4, stride=4), :]
                emit(n, o4, (0, r4, slice(None), cols), z4)
                z4_ref[n, t, r4] = z4
                for e in range(4):
                    z16 = z4_ref[n, t, r4, pl.ds(e, ROW_TILE // 16, stride=4), :]
                    emit(n, o16, (0, r4 + 4 * e, slice(None), cols), z16)
    xr_ref[...] = proj(3 * D_ATTN, D_RNN)
    gr_ref[...] = proj(3 * D_ATTN + D_RNN, D_RNN)


def _ffn_inproj_call(x, mod9, g, w_in, w_out, w_mix):
    s = x.shape[0]
    row = pl.BlockSpec((ROW_TILE, D_MODEL), lambda i: (i, 0))
    nat = pl.BlockSpec((ROW_TILE, D_ATTN), lambda i: (i, 0))
    c4 = pl.BlockSpec((1, 4, ROW_TILE // 4, D_ATTN), lambda i: (i, 0, 0, 0))
    per16 = ATTN_CHUNK // ROW_TILE
    c16 = pl.BlockSpec((1, 16, ROW_TILE // 16, D_ATTN),
                       lambda i: (i // per16, 0, i % per16, 0))
    s_nat = jax.ShapeDtypeStruct((s, D_ATTN), BF16)
    s_c4 = jax.ShapeDtypeStruct((s // ROW_TILE, 4, ROW_TILE // 4, D_ATTN), BF16)
    s_c16 = jax.ShapeDtypeStruct((s // ATTN_CHUNK, 16, ATTN_BLOCK, D_ATTN), BF16)
    s_f32 = jax.ShapeDtypeStruct((s, D_RNN), F32)
    return pl.pallas_call(
        _ffn_inproj_kernel,
        out_shape=([jax.ShapeDtypeStruct(x.shape, F32)]
                   + [s_nat] * 3 + [s_c4] * 4 + [s_c16] * 4 + [s_f32] * 2),
        grid=(s // ROW_TILE,),
        in_specs=[row, _resident(mod9.shape), _resident(g.shape),
                  _resident(w_in.shape), _resident(w_out.shape),
                  _resident(w_mix.shape)],
        out_specs=[row] + [nat] * 3 + [c4] * 4 + [c16] * 4 + [nat] * 2,
        scratch_shapes=[pltpu.VMEM((ROW_TILE, D_MODEL), F32),
                        pltpu.VMEM((3, D_ATTN // LANES, ROW_TILE, LANES), F32),
                        pltpu.VMEM((3, D_ATTN // LANES, 4, ROW_TILE // 4, LANES), F32)],
        compiler_params=pltpu.CompilerParams(
            dimension_semantics=("parallel",), vmem_limit_bytes=58 * MIB),
        name="ffn_inproj",
    )(x, mod9, g, w_in, w_out, w_mix)


def _attn_kernel(k1, va1, vb1, k1p, va1p, vb1p,
                 q4, k4, va4, vb4, k4p, va4p, vb4p,
                 q16, k16, va16, vb16, k16p, va16p, vb16p, o_ref,
                 kc1, vc1, kc4, vc4, kc16, vc16,
                 bias_ref, s_ref, p_ref, max_ref, acc_ref, den_ref, nat_ref):
    chunk = pl.program_id(0)
    nblk = ATTN_CHUNK // ATTN_BLOCK
    sub_rows = 4 * ATTN_BLOCK
    quarter = ATTN_BLOCK // 4
    lane = lax.broadcasted_iota(jnp.int32, (1, PAIR), 1)
    head_mask = [(lane < HEAD_DIM).astype(BF16), (lane >= HEAD_DIM).astype(BF16)]

    @pl.when((chunk == 0) & (pl.program_id(1) == 0))
    def _():
        for vc in (vc1, vc4, vc16):
            for h in range(2):
                vc[h, :, :, PAIR:] = jnp.broadcast_to(
                    head_mask[h], vc.shape[1:3] + (PAIR,))
        row = lax.broadcasted_iota(jnp.int32, (ATTN_BLOCK, 2 * ATTN_BLOCK), 0)
        col = lax.broadcasted_iota(jnp.int32, (ATTN_BLOCK, 2 * ATTN_BLOCK), 1)
        row4 = ((row & (quarter - 1)) << 2) | (row >> 5)
        for kind, r in enumerate((row, row4)):
            band = (col >= r) & (col <= r + N_BACK)
            bias_ref[2 * kind] = jnp.where(band, 0.0, -jnp.inf)
            bias_ref[2 * kind + 1] = jnp.where(band & (col >= ATTN_BLOCK), 0.0, -jnp.inf)

    def put(kc, vc, cls, rows, k, va, vb):
        kc[cls, rows] = k
        vc[0, cls, rows, :PAIR] = va
        vc[1, cls, rows, :PAIR] = vb

    prev = slice(0, ATTN_BLOCK)
    put(kc1, vc1, 0, prev, k1p[...], va1p[...], vb1p[...])
    put(kc1, vc1, 0, slice(ATTN_BLOCK, None), k1[...], va1[...], vb1[...])
    for r in range(4):
        put(kc4, vc4, r, prev, k4p[0, r], va4p[0, r], vb4p[0, r])
        for i in range(4):
            rows = slice((i + 1) * ATTN_BLOCK, (i + 2) * ATTN_BLOCK)
            put(kc4, vc4, r, rows, k4[i, r], va4[i, r], vb4[i, r])
    for r in range(16):
        put(kc16, vc16, r, prev, k16p[0, r], va16p[0, r], vb16p[0, r])
        put(kc16, vc16, r, slice(ATTN_BLOCK, None), k16[0, r], va16[0, r], vb16[0, r])

    max_ref[...] = jnp.full(max_ref.shape, -jnp.inf, F32)
    acc_ref[...] = jnp.zeros(acc_ref.shape, F32)
    den_ref[...] = jnp.zeros(den_ref.shape, F32)

    def window(j):
        return pl.ds(j * ATTN_BLOCK, 2 * ATTN_BLOCK)

    def dilation1(j):
        i, jj = divmod(j, 4)
        rows = slice(jj * quarter, (jj + 1) * quarter)
        q = jnp.concatenate([q4[i, r4, rows, :] for r4 in range(4)], axis=0)
        pieces = [(pl.ds(i * sub_rows + r4 * ATTN_BLOCK + jj * quarter, quarter),
                   slice(r4 * quarter, (r4 + 1) * quarter)) for r4 in range(4)]
        return (q, lambda: kc1[0, window(j), :], lambda h: vc1[h, 0, window(j), :],
                True if j > 0 else chunk > 0, 1, pieces)

    def dilation4(t):
        i, r4 = divmod(t, 4)
        pieces = [(pl.ds(i * sub_rows + r4 * ATTN_BLOCK, ATTN_BLOCK),
                   slice(0, ATTN_BLOCK))]
        return (q4[i, r4], lambda: kc4[r4, window(i), :],
                lambda h: vc4[h, r4, window(i), :],
                True if i > 0 else chunk > 0, 0, pieces)

    def dilation16(r):
        e, r4 = divmod(r, 4)
        pieces = [(pl.ds(i * sub_rows + r4 * ATTN_BLOCK + e, quarter, stride=4),
                   slice(i * quarter, (i + 1) * quarter)) for i in range(4)]
        return (q16[0, r], lambda: kc16[r], lambda h: vc16[h, r],
                chunk > 0, 0, pieces)

    makers = [dilation4] * nblk + [dilation16] * nblk + [dilation1] * nblk
    n_tiles = len(makers)

    def scores(t):
        q, keys, _, has_prev, kind, pieces = makers[t](t % nblk)
        no_prev = 0 if has_prev is True else jnp.where(has_prev, 0, 1)
        bias = bias_ref[2 * kind + no_prev]
        kk = keys()
        for h in range(2):
            s = lax.dot_general(q * head_mask[h], kk, (((1,), (1,)), ((), ())),
                                preferred_element_type=F32) + bias
            s_ref[t, h] = s
            m = jnp.broadcast_to(jnp.max(s, axis=-1, keepdims=True),
                                 (ATTN_BLOCK, PAIR))
            for canon, rows in pieces:
                max_ref[h, canon, :] = jnp.maximum(max_ref[h, canon, :], m[rows])

    def softmax(t):
        pieces = makers[t](t % nblk)[5]
        for h in range(2):
            m = jnp.concatenate([max_ref[h, canon, :] for canon, _ in pieces], axis=0)
            p = jnp.exp2(s_ref[t, h] - jnp.concatenate([m, m], axis=1))
            p_ref[t % nblk, h] = p.astype(BF16)

    def values(t):
        _, _, vals, _, _, pieces = makers[t](t % nblk)
        pv = (jnp.dot(p_ref[t % nblk, 0], vals(0), preferred_element_type=F32)
              + jnp.dot(p_ref[t % nblk, 1], vals(1), preferred_element_type=F32))
        for canon, rows in pieces:
            acc_ref[canon, :] += pv[rows, :PAIR]
            den_ref[canon, :] += pv[rows, PAIR:]

    def pipeline(stages):
        n_groups = n_tiles // ATTN_GROUP
        for i in range(n_groups + len(stages) - 1):
            for lag in reversed(range(len(stages))):
                if 0 <= i - lag < n_groups:
                    for j in range(ATTN_GROUP):
                        stages[lag]((i - lag) * ATTN_GROUP + j)

    pipeline([scores])
    pipeline([softmax, values])

    for i in range(ATTN_CHUNK // sub_rows):
        for r4 in range(4):
            canon = pl.ds(i * sub_rows + r4 * ATTN_BLOCK, ATTN_BLOCK)
            nat_ref[pl.ds(i * sub_rows + r4, ATTN_BLOCK, stride=4), :] = (
                acc_ref[canon, :] / den_ref[canon, :])
    o_ref[...] = nat_ref[...].astype(BF16)


def _attn_call(k1, va1, vb1, q4, k4, va4, vb4, q16, k16, va16, vb16):
    s = k1.shape[0]
    n_pairs = D_ATTN // PAIR
    blocks_per_chunk = ATTN_CHUNK // ATTN_BLOCK
    sub_per_chunk = ATTN_CHUNK // (4 * ATTN_BLOCK)

    nat = pl.BlockSpec((ATTN_CHUNK, PAIR), lambda c, p: (c, p))
    nat_prev = pl.BlockSpec(
        (ATTN_BLOCK, PAIR),
        lambda c, p: (jnp.maximum(c * blocks_per_chunk - 1, 0), p))
    c4 = pl.BlockSpec((sub_per_chunk, 4, ATTN_BLOCK, PAIR), lambda c, p: (c, 0, 0, p))
    c4_prev = pl.BlockSpec(
        (1, 4, ATTN_BLOCK, PAIR),
        lambda c, p: (jnp.maximum(c * sub_per_chunk - 1, 0), 0, 0, p))
    c16 = pl.BlockSpec((1, 16, ATTN_BLOCK, PAIR), lambda c, p: (c, 0, 0, p))
    c16_prev = pl.BlockSpec((1, 16, ATTN_BLOCK, PAIR),
                            lambda c, p: (jnp.maximum(c - 1, 0), 0, 0, p))
    kcat = lambda classes, blocks: pltpu.VMEM(
        (classes, (blocks + 1) * ATTN_BLOCK, PAIR), BF16)
    vcat = lambda classes, blocks: pltpu.VMEM(
        (2, classes, (blocks + 1) * ATTN_BLOCK, 2 * PAIR), BF16)
    rows = lambda: pltpu.VMEM((ATTN_CHUNK, PAIR), F32)
    return pl.pallas_call(
        _attn_kernel,
        out_shape=jax.ShapeDtypeStruct((s, D_ATTN), BF16),
        grid=(s // ATTN_CHUNK, n_pairs),
        in_specs=[nat] * 3 + [nat_prev] * 3 + [c4] * 4 + [c4_prev] * 3
                 + [c16] * 4 + [c16_prev] * 3,
        out_specs=nat,
        scratch_shapes=[
            kcat(1, 16), vcat(1, 16), kcat(4, 4), vcat(4, 4), kcat(16, 1), vcat(16, 1),
            pltpu.VMEM((4, ATTN_BLOCK, 2 * ATTN_BLOCK), F32),
            pltpu.VMEM((3 * blocks_per_chunk, 2, ATTN_BLOCK, 2 * ATTN_BLOCK), F32),
            pltpu.VMEM((blocks_per_chunk, 2, ATTN_BLOCK, 2 * ATTN_BLOCK), BF16),
            pltpu.VMEM((2, ATTN_CHUNK, PAIR), F32),
            rows(), rows(), rows()],
        compiler_params=pltpu.CompilerParams(
            dimension_semantics=("arbitrary", "arbitrary"),
            vmem_limit_bytes=56 * MIB),
        name="attn",
    )(k1, va1, vb1, k1, va1, vb1, q4, k4, va4, vb4, k4, va4, vb4,
      q16, k16, va16, vb16, k16, va16, vb16)


def _rglru_kernel(xr_ref, xp_ref, gr_ref, cw_ref, cb_ref, wx_ref, wa_ref,
                  bx_ref, ba_ref, lam_ref, o_ref,
                  xcat_ref, a_ref, b_ref, h_ref, carry_ref):
    step = pl.program_id(0)
    rows = xr_ref.shape[0]
    half = D_RNN // 2

    @pl.when(step == 0)
    def _():
        carry_ref[...] = jnp.zeros_like(carry_ref)

    xcat_ref[0:8, :] = jnp.where(step > 0, xp_ref[...], 0.0)
    xcat_ref[8:, :] = xr_ref[...]
    u = cb_ref[...]
    for k in range(CONV_WIDTH):
        u = u + cw_ref[k:k + 1, :] * xcat_ref[pl.ds(8 - (CONV_WIDTH - 1) + k, rows), :]

    ub = u.astype(BF16)

    def blockdiag(w_ref):
        lo = jnp.dot(ub[:, :half], w_ref[0], preferred_element_type=F32)
        hi = jnp.dot(ub[:, half:], w_ref[1], preferred_element_type=F32)
        return jnp.concatenate([lo, hi], axis=1)

    gate_x = jax.nn.sigmoid(blockdiag(wx_ref) + bx_ref[...])
    gate_a = jax.nn.sigmoid(blockdiag(wa_ref) + ba_ref[...])
    neg_lam = -lam_ref[...]
    softplus = jnp.maximum(neg_lam, 0.0) + jnp.log1p(jnp.exp(-jnp.abs(neg_lam)))
    log_a = -LRU_C * gate_a * softplus
    a = jnp.exp(log_a)
    a_ref[...] = a
    b_ref[...] = u * gate_x * jnp.sqrt(-jnp.tanh(log_a) * (a * a + 1.0))

    sub = lax.broadcasted_iota(jnp.int32, (8, D_RNN), 0)

    def group(gi, h_prev):
        r0 = pl.multiple_of(gi * 8, 8)
        ga = a_ref[pl.ds(r0, 8), :]
        gb = b_ref[pl.ds(r0, 8), :]
        for s in (1, 2, 4):
            a_sh = jnp.where(sub < s, 1.0, pltpu.roll(ga, s, 0))
            b_sh = jnp.where(sub < s, 0.0, pltpu.roll(gb, s, 0))
            gb = ga * b_sh + gb
            ga = ga * a_sh
        h = ga * h_prev + gb
        h_ref[pl.ds(r0, 8), :] = h
        return jnp.broadcast_to(h[7:8, :], (8, D_RNN))

    carry_ref[...] = lax.fori_loop(0, rows // 8, group, carry_ref[...], unroll=8)

    g = gr_ref[...]
    gelu = 0.5 * g * (1.0 + jnp.tanh(math.sqrt(2.0 / math.pi) * (g + 0.044715 * g * g * g)))
    o_ref[...] = (h_ref[...] * gelu).astype(BF16)


def _rglru_call(xr, gr, conv_w, conv_b, wx, wa, b_x, b_a, lam):
    s = xr.shape[0]
    row = pl.BlockSpec((ROW_TILE, D_RNN), lambda i: (i, 0))
    prev8 = pl.BlockSpec((8, D_RNN),
                         lambda i: (jnp.maximum(i * (ROW_TILE // 8) - 1, 0), 0))
    vec = _resident((1, D_RNN))
    return pl.pallas_call(
        _rglru_kernel,
        out_shape=jax.ShapeDtypeStruct((s, D_RNN), BF16),
        grid=(s // ROW_TILE,),
        in_specs=[row, prev8, row, _resident(conv_w.shape), vec,
                  _resident(wx.shape), _resident(wa.shape), vec, vec, vec],
        out_specs=row,
        scratch_shapes=[pltpu.VMEM((ROW_TILE + 8, D_RNN), F32),
                        pltpu.VMEM((ROW_TILE, D_RNN), F32),
                        pltpu.VMEM((ROW_TILE, D_RNN), F32),
                        pltpu.VMEM((ROW_TILE, D_RNN), F32),
                        pltpu.VMEM((8, D_RNN), F32)],
        compiler_params=pltpu.CompilerParams(
            dimension_semantics=("arbitrary",), vmem_limit_bytes=32 * MIB),
        name="rglru",
    )(xr, xr, gr, conv_w, conv_b, wx, wa, b_x, b_a, lam)


def _outproj_ffn_kernel(x_ref, attn_ref, rec_ref, mod_ref, g_ref, wmix_ref,
                        win_ref, wout_ref, o_ref, acc_ref):
    y = jnp.dot(attn_ref[...], wmix_ref[:D_ATTN, :], preferred_element_type=F32)
    y = y + jnp.dot(rec_ref[...], wmix_ref[D_ATTN:, :], preferred_element_type=F32)
    x2 = _post_residual(x_ref[...], y, mod_ref, g_ref, 1, 1.0)
    o_ref[...] = _ffn(2, x2, mod_ref, g_ref, win_ref, wout_ref, acc_ref)


def _outproj_ffn_call(x, attn, rec, mod9, g, w_mix, w_in, w_out):
    row = pl.BlockSpec((ROW_TILE, D_MODEL), lambda i: (i, 0))
    half = pl.BlockSpec((ROW_TILE, D_ATTN), lambda i: (i, 0))
    return pl.pallas_call(
        _outproj_ffn_kernel,
        out_shape=jax.ShapeDtypeStruct(x.shape, F32),
        grid=(x.shape[0] // ROW_TILE,),
        in_specs=[row, half, half, _resident(mod9.shape), _resident(g.shape),
                  _resident(w_mix.shape), _resident(w_in.shape),
                  _resident(w_out.shape)],
        out_specs=row,
        scratch_shapes=[pltpu.VMEM((ROW_TILE, D_MODEL), F32)],
        compiler_params=pltpu.CompilerParams(
            dimension_semantics=("parallel",), vmem_limit_bytes=52 * MIB),
        name="outproj_ffn",
    )(x, attn, rec, mod9, g, w_mix, w_in, w_out)


def _blockdiag_halves(w):
    per_half = (D_RNN // 2) // LRU_BLOCK
    eye = jnp.eye(per_half, dtype=w.dtype)
    wh = w.reshape(2, per_half, LRU_BLOCK, LRU_BLOCK)
    dense = jnp.einsum('hgij,gk->hgikj', wh, eye)
    return dense.reshape(2, D_RNN // 2, D_RNN // 2).astype(BF16)


def kernel(x, c, w_ada, b_ada, norm_gain, ffn1_w_in, ffn1_w_out, mix_w_in, conv_w, conv_b,
           lru_w_x, lru_b_x, lru_w_a, lru_b_a, lru_lambda, mix_w_out, ffn2_w_in, ffn2_w_out):
    batch, seq, d = x.shape
    assert (batch, seq, d) == (1, SEQ, D_MODEL) and w_ada.shape[0] == 1
    xs = x.reshape(seq, d)
    g = norm_gain[0]

    mod = _mod_call(c.reshape(d, 1), w_ada[0], b_ada)
    mod9 = mod.reshape(9, d)

    x1, *qkv, xr, gr = _ffn_inproj_call(
        xs, mod9, g, ffn1_w_in[0].astype(BF16), ffn1_w_out[0].astype(BF16),
        mix_w_in[0].astype(BF16))
    attn = _attn_call(*qkv)
    rec = _rglru_call(xr, gr, conv_w[0], conv_b, _blockdiag_halves(lru_w_x[0]),
                      _blockdiag_halves(lru_w_a[0]), lru_b_x, lru_b_a, lru_lambda)
    x3 = _outproj_ffn_call(x1, attn, rec, mod9, g, mix_w_out[0].astype(BF16),
                           ffn2_w_in[0].astype(BF16), ffn2_w_out[0].astype(BF16))
    return x3.reshape(batch, seq, d)
```

```python
import math

import jax
import jax.numpy as jnp
from jax import lax
from jax.experimental import pallas as pl
from jax.experimental.pallas import tpu as pltpu

F32 = jnp.float32
BF16 = jnp.bfloat16

D_MODEL = 1024
SEQ = 16384
HEAD_DIM = 64
D_ATTN = 512
D_RNN = 512
LRU_BLOCK = 64
N_BACK = 128
ATTN_BLOCK = 128
CONV_WIDTH = 4
LRU_C = 8.0
D_FF = 2816
FFN_RES_WEIGHT = 0.5
RMS_EPS = 1e-6

MIB = 1024 * 1024
ROW_TILE = 512
SEG_LEN = ROW_TILE // 8
FF_CHUNK = 256
N_FF_CHUNKS = D_FF // FF_CHUNK
ATTN_CHUNK = 2048
LANES = 128
PAIR = 2 * HEAD_DIM
MOD_TILE = 1152
ATTN_GROUP = 4


def _resident(shape):
    return pl.BlockSpec(shape, lambda *_: (0,) * len(shape),
                        pipeline_mode=pl.Buffered(1))


def _rms(v):
    return v * lax.rsqrt(jnp.mean(v * v, axis=-1, keepdims=True) + RMS_EPS)


def _pre(x, mod_ref, g_ref, sub):
    shift = mod_ref[3 * sub:3 * sub + 1, :]
    scale = mod_ref[3 * sub + 1:3 * sub + 2, :]
    gain = g_ref[2 * sub:2 * sub + 1, :] * (1.0 + scale)
    return _rms(x) * gain + shift


def _post_residual(x, y, mod_ref, g_ref, sub, weight):
    gate = mod_ref[3 * sub + 2:3 * sub + 3, :]
    gain = weight * (1.0 + gate) * g_ref[2 * sub + 1:2 * sub + 2, :]
    return x + _rms(y) * gain


def _mod_kernel(c_ref, w_ref, b_ref, o_ref):
    c = c_ref[...]
    s = c * jax.nn.sigmoid(c)
    o_ref[...] = jnp.sum(s * w_ref[...], axis=0, keepdims=True) + b_ref[...]


def _mod_call(c_col, w_ada, b_ada):
    n = w_ada.shape[1]
    return pl.pallas_call(
        _mod_kernel,
        out_shape=jax.ShapeDtypeStruct((1, n), F32),
        grid=(n // MOD_TILE,),
        in_specs=[
            pl.BlockSpec((D_MODEL, 1), lambda j: (0, 0)),
            pl.BlockSpec((D_MODEL, MOD_TILE), lambda j: (0, j)),
            pl.BlockSpec((1, MOD_TILE), lambda j: (0, j)),
        ],
        out_specs=pl.BlockSpec((1, MOD_TILE), lambda j: (0, j)),
        compiler_params=pltpu.CompilerParams(
            dimension_semantics=("arbitrary",), vmem_limit_bytes=24 * MIB),
        name="mod",
    )(c_col, w_ada, b_ada)


def _ffn(sub, x, mod_ref, g_ref, win_ref, wout_ref, acc_ref):
    h = _pre(x, mod_ref, g_ref, sub).astype(BF16)

    def gate_up(c):
        lo = c * FF_CHUNK
        return (jnp.dot(h, win_ref[:, lo:lo + FF_CHUNK], preferred_element_type=F32),
                jnp.dot(h, win_ref[:, D_FF + lo:D_FF + lo + FF_CHUNK],
                        preferred_element_type=F32))

    hg, hu = gate_up(0)
    for c in range(N_FF_CHUNKS):
        a = (hg * jax.nn.sigmoid(hg) * hu).astype(BF16)
        if c + 1 < N_FF_CHUNKS:
            hg, hu = gate_up(c + 1)
        part = jnp.dot(a, wout_ref[c * FF_CHUNK:(c + 1) * FF_CHUNK, :],
                       preferred_element_type=F32)
        if c == 0:
            acc_ref[...] = part
        else:
            acc_ref[...] += part
    return _post_residual(x, acc_ref[...], mod_ref, g_ref, sub, FFN_RES_WEIGHT)


def _ffn_inproj_kernel(x_ref, mod_ref, g_ref, win_ref, wout_ref, wmix_ref,
                       x1_ref, k1, va1, vb1, q4, k4, va4, vb4,
                       q16, k16, va16, vb16, xr_ref, gr_ref,
                       acc_ref, z_ref, z4_ref):
    x1 = _ffn(0, x_ref[...], mod_ref, g_ref, win_ref, wout_ref, acc_ref)
    x1_ref[...] = x1
    h = _pre(x1, mod_ref, g_ref, 1).astype(BF16)

    def proj(j, width):
        return jnp.dot(h, wmix_ref[:, j:j + width], preferred_element_type=F32)

    first_head = lax.broadcasted_iota(jnp.int32, (1, PAIR), 1) < HEAD_DIM

    def emit(n, dsts, idx, tile):
        if n < 2:
            dsts[0][idx] = tile.astype(BF16)
        else:
            dsts[0][idx] = jnp.where(first_head, tile, 0.0).astype(BF16)
            dsts[1][idx] = jnp.where(first_head, 0.0, tile).astype(BF16)

    outs = ((None, (q4,), (q16,)), ((k1,), (k4,), (k16,)),
            ((va1, vb1), (va4, vb4), (va16, vb16)))
    for n, (o1, o4, o16) in enumerate(outs):
        z = proj(n * D_ATTN, D_ATTN)
        if n == 0:
            z = z * (HEAD_DIM ** -0.5 * math.log2(math.e))
        for t in range(D_ATTN // LANES):
            cols = slice(t * LANES, (t + 1) * LANES)
            if o1 is not None:
                emit(n, o1, (slice(None), cols), z[:, cols])
            z_ref[n, t] = z[:, cols]
            for r4 in range(4):
                z4 = z_ref[n, t, pl.ds(r4, ROW_TILE // 4, stride=4), :]
                emit(n, o4, (0, r4, slice(None), cols), z4)
                z4_ref[n, t, r4] = z4
                for e in range(4):
                    z16 = z4_ref[n, t, r4, pl.ds(e, ROW_TILE // 16, stride=4), :]
                    emit(n, o16, (0, r4 + 4 * e, slice(None), cols), z16)

    for n, dst in enumerate((xr_ref, gr_ref)):
        z = proj(3 * D_ATTN + n * D_RNN, D_RNN)
        for t in range(D_RNN // LANES):
            for s in range(8):
                dst[0, t, pl.ds(s, SEG_LEN, stride=8), :] = z[
                    s * SEG_LEN:(s + 1) * SEG_LEN, t * LANES:(t + 1) * LANES]


def _ffn_inproj_call(x, mod9, g, w_in, w_out, w_mix):
    s = x.shape[0]
    row = pl.BlockSpec((ROW_TILE, D_MODEL), lambda i: (i, 0))
    nat = pl.BlockSpec((ROW_TILE, D_ATTN), lambda i: (i, 0))
    c4 = pl.BlockSpec((1, 4, ROW_TILE // 4, D_ATTN), lambda i: (i, 0, 0, 0))
    per16 = ATTN_CHUNK // ROW_TILE
    c16 = pl.BlockSpec((1, 16, ROW_TILE // 16, D_ATTN),
                       lambda i: (i // per16, 0, i % per16, 0))
    s_nat = jax.ShapeDtypeStruct((s, D_ATTN), BF16)
    s_c4 = jax.ShapeDtypeStruct((s // ROW_TILE, 4, ROW_TILE // 4, D_ATTN), BF16)
    s_c16 = jax.ShapeDtypeStruct((s // ATTN_CHUNK, 16, ATTN_BLOCK, D_ATTN), BF16)
    seg = pl.BlockSpec((1, D_RNN // LANES, ROW_TILE, LANES), lambda i: (i, 0, 0, 0))
    s_seg = jax.ShapeDtypeStruct((s // ROW_TILE, D_RNN // LANES, ROW_TILE, LANES), F32)
    return pl.pallas_call(
        _ffn_inproj_kernel,
        out_shape=([jax.ShapeDtypeStruct(x.shape, F32)]
                   + [s_nat] * 3 + [s_c4] * 4 + [s_c16] * 4 + [s_seg] * 2),
        grid=(s // ROW_TILE,),
        in_specs=[row, _resident(mod9.shape), _resident(g.shape),
                  _resident(w_in.shape), _resident(w_out.shape),
                  _resident(w_mix.shape)],
        out_specs=[row] + [nat] * 3 + [c4] * 4 + [c16] * 4 + [seg] * 2,
        scratch_shapes=[pltpu.VMEM((ROW_TILE, D_MODEL), F32),
                        pltpu.VMEM((3, D_ATTN // LANES, ROW_TILE, LANES), F32),
                        pltpu.VMEM((3, D_ATTN // LANES, 4, ROW_TILE // 4, LANES), F32)],
        compiler_params=pltpu.CompilerParams(
            dimension_semantics=("parallel",), vmem_limit_bytes=58 * MIB),
        name="ffn_inproj",
    )(x, mod9, g, w_in, w_out, w_mix)


def _attn_kernel(k1, va1, vb1, k1p, va1p, vb1p,
                 q4, k4, va4, vb4, k4p, va4p, vb4p,
                 q16, k16, va16, vb16, k16p, va16p, vb16p, o_ref,
                 kc1, vc1, kc4, vc4, kc16, vc16,
                 bias_ref, s_ref, p_ref, max_ref, acc_ref, den_ref, nat_ref):
    chunk = pl.program_id(0)
    nblk = ATTN_CHUNK // ATTN_BLOCK
    sub_rows = 4 * ATTN_BLOCK
    quarter = ATTN_BLOCK // 4
    lane = lax.broadcasted_iota(jnp.int32, (1, PAIR), 1)
    head_mask = [(lane < HEAD_DIM).astype(BF16), (lane >= HEAD_DIM).astype(BF16)]

    @pl.when((chunk == 0) & (pl.program_id(1) == 0))
    def _():
        for vc in (vc1, vc4, vc16):
            for h in range(2):
                vc[h, :, :, PAIR:] = jnp.broadcast_to(
                    head_mask[h], vc.shape[1:3] + (PAIR,))
        row = lax.broadcasted_iota(jnp.int32, (ATTN_BLOCK, 2 * ATTN_BLOCK), 0)
        col = lax.broadcasted_iota(jnp.int32, (ATTN_BLOCK, 2 * ATTN_BLOCK), 1)
        row4 = ((row & (quarter - 1)) << 2) | (row >> 5)
        for kind, r in enumerate((row, row4)):
            band = (col >= r) & (col <= r + N_BACK)
            bias_ref[2 * kind] = jnp.where(band, 0.0, -jnp.inf)
            bias_ref[2 * kind + 1] = jnp.where(band & (col >= ATTN_BLOCK), 0.0, -jnp.inf)

    def put(kc, vc, cls, rows, k, va, vb):
        kc[cls, rows] = k
        vc[0, cls, rows, :PAIR] = va
        vc[1, cls, rows, :PAIR] = vb

    prev = slice(0, ATTN_BLOCK)
    put(kc1, vc1, 0, prev, k1p[...], va1p[...], vb1p[...])
    put(kc1, vc1, 0, slice(ATTN_BLOCK, None), k1[...], va1[...], vb1[...])
    for r in range(4):
        put(kc4, vc4, r, prev, k4p[0, r], va4p[0, r], vb4p[0, r])
        for i in range(4):
            rows = slice((i + 1) * ATTN_BLOCK, (i + 2) * ATTN_BLOCK)
            put(kc4, vc4, r, rows, k4[i, r], va4[i, r], vb4[i, r])
    for r in range(16):
        put(kc16, vc16, r, prev, k16p[0, r], va16p[0, r], vb16p[0, r])
        put(kc16, vc16, r, slice(ATTN_BLOCK, None), k16[0, r], va16[0, r], vb16[0, r])

    max_ref[...] = jnp.full(max_ref.shape, -jnp.inf, F32)
    acc_ref[...] = jnp.zeros(acc_ref.shape, F32)
    den_ref[...] = jnp.zeros(den_ref.shape, F32)

    def window(j):
        return pl.ds(j * ATTN_BLOCK, 2 * ATTN_BLOCK)

    def dilation1(j):
        i, jj = divmod(j, 4)
        rows = slice(jj * quarter, (jj + 1) * quarter)
        q = jnp.concatenate([q4[i, r4, rows, :] for r4 in range(4)], axis=0)
        pieces = [(pl.ds(i * sub_rows + r4 * ATTN_BLOCK + jj * quarter, quarter),
                   slice(r4 * quarter, (r4 + 1) * quarter)) for r4 in range(4)]
        return (q, lambda: kc1[0, window(j), :], lambda h: vc1[h, 0, window(j), :],
                True if j > 0 else chunk > 0, 1, pieces)

    def dilation4(t):
        i, r4 = divmod(t, 4)
        pieces = [(pl.ds(i * sub_rows + r4 * ATTN_BLOCK, ATTN_BLOCK),
                   slice(0, ATTN_BLOCK))]
        return (q4[i, r4], lambda: kc4[r4, window(i), :],
                lambda h: vc4[h, r4, window(i), :],
                True if i > 0 else chunk > 0, 0, pieces)

    def dilation16(r):
        e, r4 = divmod(r, 4)
        pieces = [(pl.ds(i * sub_rows + r4 * ATTN_BLOCK + e, quarter, stride=4),
                   slice(i * quarter, (i + 1) * quarter)) for i in range(4)]
        return (q16[0, r], lambda: kc16[r], lambda h: vc16[h, r],
                chunk > 0, 0, pieces)

    makers = [dilation4] * nblk + [dilation16] * nblk + [dilation1] * nblk
    n_tiles = len(makers)

    def scores(t):
        q, keys, _, has_prev, kind, pieces = makers[t](t % nblk)
        no_prev = 0 if has_prev is True else jnp.where(has_prev, 0, 1)
        bias = bias_ref[2 * kind + no_prev]
        kk = keys()
        for h in range(2):
            s = lax.dot_general(q * head_mask[h], kk, (((1,), (1,)), ((), ())),
                                preferred_element_type=F32) + bias
            s_ref[t, h] = s
            m = jnp.broadcast_to(jnp.max(s, axis=-1, keepdims=True),
                                 (ATTN_BLOCK, PAIR))
            for canon, rows in pieces:
                max_ref[h, canon, :] = jnp.maximum(max_ref[h, canon, :], m[rows])

    def softmax(t):
        pieces = makers[t](t % nblk)[5]
        for h in range(2):
            m = jnp.concatenate([max_ref[h, canon, :] for canon, _ in pieces], axis=0)
            p = jnp.exp2(s_ref[t, h] - jnp.concatenate([m, m], axis=1))
            p_ref[t % nblk, h] = p.astype(BF16)

    def values(t):
        _, _, vals, _, _, pieces = makers[t](t % nblk)
        pv = (jnp.dot(p_ref[t % nblk, 0], vals(0), preferred_element_type=F32)
              + jnp.dot(p_ref[t % nblk, 1], vals(1), preferred_element_type=F32))
        for canon, rows in pieces:
            acc_ref[canon, :] += pv[rows, :PAIR]
            den_ref[canon, :] += pv[rows, PAIR:]

    def pipeline(stages):
        n_groups = n_tiles // ATTN_GROUP
        for i in range(n_groups + len(stages) - 1):
            for lag in reversed(range(len(stages))):
                if 0 <= i - lag < n_groups:
                    for j in range(ATTN_GROUP):
                        stages[lag]((i - lag) * ATTN_GROUP + j)

    pipeline([scores])
    pipeline([softmax, values])

    for i in range(ATTN_CHUNK // sub_rows):
        for r4 in range(4):
            canon = pl.ds(i * sub_rows + r4 * ATTN_BLOCK, ATTN_BLOCK)
            nat_ref[pl.ds(i * sub_rows + r4, ATTN_BLOCK, stride=4), :] = (
                acc_ref[canon, :] / den_ref[canon, :])
    o_ref[...] = nat_ref[...].astype(BF16)


def _attn_call(k1, va1, vb1, q4, k4, va4, vb4, q16, k16, va16, vb16):
    s = k1.shape[0]
    n_pairs = D_ATTN // PAIR
    blocks_per_chunk = ATTN_CHUNK // ATTN_BLOCK
    sub_per_chunk = ATTN_CHUNK // (4 * ATTN_BLOCK)

    nat = pl.BlockSpec((ATTN_CHUNK, PAIR), lambda c, p: (c, p))
    nat_prev = pl.BlockSpec(
        (ATTN_BLOCK, PAIR),
        lambda c, p: (jnp.maximum(c * blocks_per_chunk - 1, 0), p))
    c4 = pl.BlockSpec((sub_per_chunk, 4, ATTN_BLOCK, PAIR), lambda c, p: (c, 0, 0, p))
    c4_prev = pl.BlockSpec(
        (1, 4, ATTN_BLOCK, PAIR),
        lambda c, p: (jnp.maximum(c * sub_per_chunk - 1, 0), 0, 0, p))
    c16 = pl.BlockSpec((1, 16, ATTN_BLOCK, PAIR), lambda c, p: (c, 0, 0, p))
    c16_prev = pl.BlockSpec((1, 16, ATTN_BLOCK, PAIR),
                            lambda c, p: (jnp.maximum(c - 1, 0), 0, 0, p))
    kcat = lambda classes, blocks: pltpu.VMEM(
        (classes, (blocks + 1) * ATTN_BLOCK, PAIR), BF16)
    vcat = lambda classes, blocks: pltpu.VMEM(
        (2, classes, (blocks + 1) * ATTN_BLOCK, 2 * PAIR), BF16)
    rows = lambda: pltpu.VMEM((ATTN_CHUNK, PAIR), F32)
    return pl.pallas_call(
        _attn_kernel,
        out_shape=jax.ShapeDtypeStruct((s, D_ATTN), BF16),
        grid=(s // ATTN_CHUNK, n_pairs),
        in_specs=[nat] * 3 + [nat_prev] * 3 + [c4] * 4 + [c4_prev] * 3
                 + [c16] * 4 + [c16_prev] * 3,
        out_specs=nat,
        scratch_shapes=[
            kcat(1, 16), vcat(1, 16), kcat(4, 4), vcat(4, 4), kcat(16, 1), vcat(16, 1),
            pltpu.VMEM((4, ATTN_BLOCK, 2 * ATTN_BLOCK), F32),
            pltpu.VMEM((3 * blocks_per_chunk, 2, ATTN_BLOCK, 2 * ATTN_BLOCK), F32),
            pltpu.VMEM((blocks_per_chunk, 2, ATTN_BLOCK, 2 * ATTN_BLOCK), BF16),
            pltpu.VMEM((2, ATTN_CHUNK, PAIR), F32),
            rows(), rows(), rows()],
        compiler_params=pltpu.CompilerParams(
            dimension_semantics=("arbitrary", "arbitrary"),
            vmem_limit_bytes=56 * MIB),
        name="attn",
    )(k1, va1, vb1, k1, va1, vb1, q4, k4, va4, vb4, k4, va4, vb4,
      q16, k16, va16, vb16, k16, va16, vb16)


def _rglru_kernel(xr_ref, xp_ref, gr_ref, cw_ref, cb_ref, wx_ref, wa_ref,
                  bx_ref, ba_ref, lam_ref, o_ref,
                  xcat_ref, a_ref, b_ref, h_ref, nat_ref, carry_ref):
    step = pl.program_id(0)
    half = D_RNN // 2
    n_slabs = D_RNN // LANES
    halo = 8 * (CONV_WIDTH - 1)

    @pl.when(step == 0)
    def _():
        carry_ref[...] = jnp.zeros_like(carry_ref)

    def wide(ref, rows):
        return jnp.concatenate([ref[0, t, rows, :] for t in range(n_slabs)], axis=1)

    sub = lax.broadcasted_iota(jnp.int32, (8, D_RNN), 0)
    for d in range(1, CONV_WIDTH):
        cur = wide(xr_ref, pl.ds(8 * (SEG_LEN - d), 8))
        prv = jnp.where(step > 0, wide(xp_ref, pl.ds(8 * (CONV_WIDTH - d), 8)), 0.0)
        xcat_ref[pl.ds(halo - 8 * d, 8), :] = jnp.where(
            sub == 0, pltpu.roll(prv, 1, 0), pltpu.roll(cur, 1, 0))
    xcat_ref[halo:, :] = wide(xr_ref, slice(None))
    u = cb_ref[...]
    for k in range(CONV_WIDTH):
        d = CONV_WIDTH - 1 - k
        u = u + cw_ref[k:k + 1, :] * xcat_ref[pl.ds(halo - 8 * d, ROW_TILE), :]

    ub = u.astype(BF16)

    def gate(w_ref, b_ref):
        lo = jnp.dot(ub[:, :half], w_ref[0], preferred_element_type=F32)
        hi = jnp.dot(ub[:, half:], w_ref[1], preferred_element_type=F32)
        z = jnp.concatenate([lo, hi], axis=1) + b_ref[...]
        return 0.5 * jnp.tanh(0.5 * z) + 0.5

    gate_x = gate(wx_ref, bx_ref)
    gate_a = gate(wa_ref, ba_ref)
    neg_lam = -lam_ref[...]
    softplus = jnp.maximum(neg_lam, 0.0) + jnp.log1p(jnp.exp(-jnp.abs(neg_lam)))
    log_a = -LRU_C * gate_a * softplus
    a = jnp.exp(log_a)
    one_minus_a2 = -jnp.tanh(log_a) * (a * a + 1.0)
    root = jnp.where(one_minus_a2 > 0.0, one_minus_a2 * lax.rsqrt(one_minus_a2), 0.0)
    a_ref[...] = a
    b_ref[...] = u * gate_x * root

    def advance(j, carry):
        h_loc, prod = carry
        rows = pl.ds(pl.multiple_of(j * 8, 8), 8)
        a_j = a_ref[rows, :]
        h_loc = a_j * h_loc + b_ref[rows, :]
        prod = a_j * prod
        h_ref[rows, :] = h_loc
        a_ref[rows, :] = prod
        return h_loc, prod

    seg_end, seg_prod = lax.fori_loop(
        0, SEG_LEN, advance,
        (jnp.zeros((8, D_RNN), F32), jnp.ones((8, D_RNN), F32)), unroll=8)

    entering = [carry_ref[0:1, :]]
    for s in range(8):
        entering.append(seg_end[s:s + 1, :] + seg_prod[s:s + 1, :] * entering[s])
    carry_ref[0:1, :] = entering[8]
    state_in = pltpu.repeat(jnp.concatenate(entering[:8], axis=0), SEG_LEN, 0)

    g = wide(gr_ref, slice(None))
    g_half = 0.5 * g
    inner = g * (math.sqrt(2.0 / math.pi)
                 + (0.044715 * math.sqrt(2.0 / math.pi)) * (g * g))
    gelu = g_half + g_half * jnp.tanh(inner)
    out = (h_ref[...] + a_ref[...] * state_in) * gelu

    for t in range(n_slabs):
        nat_ref[t] = out[:, t * LANES:(t + 1) * LANES]
        for s in range(8):
            o_ref[s * SEG_LEN:(s + 1) * SEG_LEN, t * LANES:(t + 1) * LANES] = (
                nat_ref[t, pl.ds(s, SEG_LEN, stride=8), :].astype(BF16))


def _rglru_call(xr, gr, conv_w, conv_b, wx, wa, b_x, b_a, lam):
    n_tiles, n_slabs = xr.shape[0], xr.shape[1]
    tile = pl.BlockSpec((1, n_slabs, ROW_TILE, LANES), lambda i: (i, 0, 0, 0))
    last_rows = 8 * CONV_WIDTH
    prev = pl.BlockSpec(
        (1, n_slabs, last_rows, LANES),
        lambda i: (jnp.maximum(i - 1, 0), 0, ROW_TILE // last_rows - 1, 0))
    vec = _resident((1, D_RNN))
    return pl.pallas_call(
        _rglru_kernel,
        out_shape=jax.ShapeDtypeStruct((n_tiles * ROW_TILE, D_RNN), BF16),
        grid=(n_tiles,),
        in_specs=[tile, prev, tile, _resident(conv_w.shape), vec,
                  _resident(wx.shape), _resident(wa.shape), vec, vec, vec],
        out_specs=pl.BlockSpec((ROW_TILE, D_RNN), lambda i: (i, 0)),
        scratch_shapes=[pltpu.VMEM((ROW_TILE + 8 * (CONV_WIDTH - 1), D_RNN), F32),
                        pltpu.VMEM((ROW_TILE, D_RNN), F32),
                        pltpu.VMEM((ROW_TILE, D_RNN), F32),
                        pltpu.VMEM((ROW_TILE, D_RNN), F32),
                        pltpu.VMEM((n_slabs, ROW_TILE, LANES), F32),
                        pltpu.VMEM((8, D_RNN), F32)],
        compiler_params=pltpu.CompilerParams(
            dimension_semantics=("arbitrary",), vmem_limit_bytes=32 * MIB),
        name="rglru",
    )(xr, xr, gr, conv_w, conv_b, wx, wa, b_x, b_a, lam)


def _outproj_ffn_kernel(x_ref, attn_ref, rec_ref, mod_ref, g_ref, wmix_ref,
                        win_ref, wout_ref, o_ref, acc_ref):
    y = jnp.dot(attn_ref[...], wmix_ref[:D_ATTN, :], preferred_element_type=F32)
    y = y + jnp.dot(rec_ref[...], wmix_ref[D_ATTN:, :], preferred_element_type=F32)
    x2 = _post_residual(x_ref[...], y, mod_ref, g_ref, 1, 1.0)
    o_ref[...] = _ffn(2, x2, mod_ref, g_ref, win_ref, wout_ref, acc_ref)


def _outproj_ffn_call(x, attn, rec, mod9, g, w_mix, w_in, w_out):
    row = pl.BlockSpec((ROW_TILE, D_MODEL), lambda i: (i, 0))
    half = pl.BlockSpec((ROW_TILE, D_ATTN), lambda i: (i, 0))
    return pl.pallas_call(
        _outproj_ffn_kernel,
        out_shape=jax.ShapeDtypeStruct(x.shape, F32),
        grid=(x.shape[0] // ROW_TILE,),
        in_specs=[row, half, half, _resident(mod9.shape), _resident(g.shape),
                  _resident(w_mix.shape), _resident(w_in.shape),
                  _resident(w_out.shape)],
        out_specs=row,
        scratch_shapes=[pltpu.VMEM((ROW_TILE, D_MODEL), F32)],
        compiler_params=pltpu.CompilerParams(
            dimension_semantics=("parallel",), vmem_limit_bytes=52 * MIB),
        name="outproj_ffn",
    )(x, attn, rec, mod9, g, w_mix, w_in, w_out)


def _blockdiag_halves(w):
    per_half = (D_RNN // 2) // LRU_BLOCK
    eye = jnp.eye(per_half, dtype=w.dtype)
    wh = w.reshape(2, per_half, LRU_BLOCK, LRU_BLOCK)
    dense = jnp.einsum('hgij,gk->hgikj', wh, eye)
    return dense.reshape(2, D_RNN // 2, D_RNN // 2).astype(BF16)


def kernel(x, c, w_ada, b_ada, norm_gain, ffn1_w_in, ffn1_w_out, mix_w_in, conv_w, conv_b,
           lru_w_x, lru_b_x, lru_w_a, lru_b_a, lru_lambda, mix_w_out, ffn2_w_in, ffn2_w_out):
    batch, seq, d = x.shape
    assert (batch, seq, d) == (1, SEQ, D_MODEL) and w_ada.shape[0] == 1
    xs = x.reshape(seq, d)
    g = norm_gain[0]

    mod = _mod_call(c.reshape(d, 1), w_ada[0], b_ada)
    mod9 = mod.reshape(9, d)

    x1, *qkv, xr, gr = _ffn_inproj_call(
        xs, mod9, g, ffn1_w_in[0].astype(BF16), ffn1_w_out[0].astype(BF16),
        mix_w_in[0].astype(BF16))
    attn = _attn_call(*qkv)
    rec = _rglru_call(xr, gr, conv_w[0], conv_b, _blockdiag_halves(lru_w_x[0]),
                      _blockdiag_halves(lru_w_a[0]), lru_b_x, lru_b_a, lru_lambda)
    x3 = _outproj_ffn_call(x1, attn, rec, mod9, g, mix_w_out[0].astype(BF16),
                           ffn2_w_in[0].astype(BF16), ffn2_w_out[0].astype(BF16))
    return x3.reshape(batch, seq, d)
```

```python
import math

import jax
import jax.numpy as jnp
from jax import lax
from jax.experimental import pallas as pl
from jax.experimental.pallas import tpu as pltpu

F32 = jnp.float32
BF16 = jnp.bfloat16

D_MODEL = 1024
SEQ = 16384
HEAD_DIM = 64
D_ATTN = 512
D_RNN = 512
LRU_BLOCK = 64
N_BACK = 128
ATTN_BLOCK = 128
CONV_WIDTH = 4
LRU_C = 8.0
D_FF = 2816
FFN_RES_WEIGHT = 0.5
RMS_EPS = 1e-6

MIB = 1024 * 1024
ROW_TILE = 512
SEG_LEN = ROW_TILE // 8
FF_CHUNK = 256
N_FF_CHUNKS = D_FF // FF_CHUNK
ATTN_CHUNK = 2048
LANES = 128
PAIR = 2 * HEAD_DIM
MOD_TILE = 1152
ATTN_GROUP = 4


def _resident(shape):
    return pl.BlockSpec(shape, lambda *_: (0,) * len(shape),
                        pipeline_mode=pl.Buffered(1))


def _rms(v):
    return v * lax.rsqrt(jnp.mean(v * v, axis=-1, keepdims=True) + RMS_EPS)


def _pre(x, mod_ref, g_ref, sub):
    shift = mod_ref[3 * sub:3 * sub + 1, :]
    scale = mod_ref[3 * sub + 1:3 * sub + 2, :]
    gain = g_ref[2 * sub:2 * sub + 1, :] * (1.0 + scale)
    return _rms(x) * gain + shift


def _post_residual(x, y, mod_ref, g_ref, sub, weight):
    gate = mod_ref[3 * sub + 2:3 * sub + 3, :]
    gain = weight * (1.0 + gate) * g_ref[2 * sub + 1:2 * sub + 2, :]
    return x + _rms(y) * gain


def _mod_kernel(c_ref, w_ref, b_ref, o_ref):
    c = c_ref[...]
    s = c * jax.nn.sigmoid(c)
    o_ref[...] = jnp.sum(s * w_ref[...], axis=0, keepdims=True) + b_ref[...]


def _mod_call(c_col, w_ada, b_ada):
    n = w_ada.shape[1]
    return pl.pallas_call(
        _mod_kernel,
        out_shape=jax.ShapeDtypeStruct((1, n), F32),
        grid=(n // MOD_TILE,),
        in_specs=[
            pl.BlockSpec((D_MODEL, 1), lambda j: (0, 0)),
            pl.BlockSpec((D_MODEL, MOD_TILE), lambda j: (0, j)),
            pl.BlockSpec((1, MOD_TILE), lambda j: (0, j)),
        ],
        out_specs=pl.BlockSpec((1, MOD_TILE), lambda j: (0, j)),
        compiler_params=pltpu.CompilerParams(
            dimension_semantics=("arbitrary",), vmem_limit_bytes=24 * MIB),
        name="mod",
    )(c_col, w_ada, b_ada)


def _ffn(sub, x, mod_ref, g_ref, win_ref, wout_ref, acc_ref):
    h = _pre(x, mod_ref, g_ref, sub).astype(BF16)

    def gate_up(c):
        lo = c * FF_CHUNK
        return (jnp.dot(h, win_ref[:, lo:lo + FF_CHUNK], preferred_element_type=F32),
                jnp.dot(h, win_ref[:, D_FF + lo:D_FF + lo + FF_CHUNK],
                        preferred_element_type=F32))

    hg, hu = gate_up(0)
    for c in range(N_FF_CHUNKS):
        a = (hg * jax.nn.sigmoid(hg) * hu).astype(BF16)
        if c + 1 < N_FF_CHUNKS:
            hg, hu = gate_up(c + 1)
        part = jnp.dot(a, wout_ref[c * FF_CHUNK:(c + 1) * FF_CHUNK, :],
                       preferred_element_type=F32)
        if c == 0:
            acc_ref[...] = part
        else:
            acc_ref[...] += part
    return _post_residual(x, acc_ref[...], mod_ref, g_ref, sub, FFN_RES_WEIGHT)


def _ffn_inproj_kernel(x_ref, mod_ref, g_ref, win_ref, wout_ref, wmix_ref,
                       x1_ref, k1, v1, q4, k4, v4, q16, k16, v16, xr_ref, gr_ref,
                       acc_ref, z_ref, z4_ref):
    x1 = _ffn(0, x_ref[...], mod_ref, g_ref, win_ref, wout_ref, acc_ref)
    x1_ref[...] = x1
    h = _pre(x1, mod_ref, g_ref, 1).astype(BF16)

    def proj(j, width):
        return jnp.dot(h, wmix_ref[:, j:j + width], preferred_element_type=F32)

    outs = ((None, q4, q16), (k1, k4, k16), (v1, v4, v16))
    for n, (o1, o4, o16) in enumerate(outs):
        z = proj(n * D_ATTN, D_ATTN)
        if n == 0:
            z = z * (HEAD_DIM ** -0.5 * math.log2(math.e))
        for t in range(D_ATTN // LANES):
            cols = slice(t * LANES, (t + 1) * LANES)
            if o1 is not None:
                o1[:, cols] = z[:, cols].astype(BF16)
            z_ref[n, t] = z[:, cols]
            for r4 in range(4):
                z4 = z_ref[n, t, pl.ds(r4, ROW_TILE // 4, stride=4), :]
                o4[0, r4, :, cols] = z4.astype(BF16)
                z4_ref[n, t, r4] = z4
                for e in range(4):
                    z16 = z4_ref[n, t, r4, pl.ds(e, ROW_TILE // 16, stride=4), :]
                    o16[0, r4 + 4 * e, :, cols] = z16.astype(BF16)

    for n, dst in enumerate((xr_ref, gr_ref)):
        z = proj(3 * D_ATTN + n * D_RNN, D_RNN)
        for t in range(D_RNN // LANES):
            for s in range(8):
                dst[0, t, pl.ds(s, SEG_LEN, stride=8), :] = z[
                    s * SEG_LEN:(s + 1) * SEG_LEN, t * LANES:(t + 1) * LANES]


def _ffn_inproj_call(x, mod9, g, w_in, w_out, w_mix):
    s = x.shape[0]
    row = pl.BlockSpec((ROW_TILE, D_MODEL), lambda i: (i, 0))
    nat = pl.BlockSpec((ROW_TILE, D_ATTN), lambda i: (i, 0))
    c4 = pl.BlockSpec((1, 4, ROW_TILE // 4, D_ATTN), lambda i: (i, 0, 0, 0))
    per16 = ATTN_CHUNK // ROW_TILE
    c16 = pl.BlockSpec((1, 16, ROW_TILE // 16, D_ATTN),
                       lambda i: (i // per16, 0, i % per16, 0))
    s_nat = jax.ShapeDtypeStruct((s, D_ATTN), BF16)
    s_c4 = jax.ShapeDtypeStruct((s // ROW_TILE, 4, ROW_TILE // 4, D_ATTN), BF16)
    s_c16 = jax.ShapeDtypeStruct((s // ATTN_CHUNK, 16, ATTN_BLOCK, D_ATTN), BF16)
    seg = pl.BlockSpec((1, D_RNN // LANES, ROW_TILE, LANES), lambda i: (i, 0, 0, 0))
    s_seg = jax.ShapeDtypeStruct((s // ROW_TILE, D_RNN // LANES, ROW_TILE, LANES), F32)
    return pl.pallas_call(
        _ffn_inproj_kernel,
        out_shape=([jax.ShapeDtypeStruct(x.shape, F32)]
                   + [s_nat] * 2 + [s_c4] * 3 + [s_c16] * 3 + [s_seg] * 2),
        grid=(s // ROW_TILE,),
        in_specs=[row, _resident(mod9.shape), _resident(g.shape),
                  _resident(w_in.shape), _resident(w_out.shape),
                  _resident(w_mix.shape)],
        out_specs=[row] + [nat] * 2 + [c4] * 3 + [c16] * 3 + [seg] * 2,
        scratch_shapes=[pltpu.VMEM((ROW_TILE, D_MODEL), F32),
                        pltpu.VMEM((3, D_ATTN // LANES, ROW_TILE, LANES), F32),
                        pltpu.VMEM((3, D_ATTN // LANES, 4, ROW_TILE // 4, LANES), F32)],
        compiler_params=pltpu.CompilerParams(
            dimension_semantics=("parallel",), vmem_limit_bytes=58 * MIB),
        name="ffn_inproj",
    )(x, mod9, g, w_in, w_out, w_mix)


def _attn_kernel(k1, v1, k1p, v1p, q4, k4, v4, k4p, v4p,
                 q16, k16, v16, k16p, v16p, o_ref,
                 kc1, vc1, kc4, vc4, kc16, vc16,
                 bias_ref, s_ref, p_ref, max_ref, acc_ref, den_ref, nat_ref):
    chunk = pl.program_id(0)
    nblk = ATTN_CHUNK // ATTN_BLOCK
    sub_rows = 4 * ATTN_BLOCK
    quarter = ATTN_BLOCK // 4
    lane = lax.broadcasted_iota(jnp.int32, (1, PAIR), 1)
    head_mask = [(lane < HEAD_DIM).astype(BF16), (lane >= HEAD_DIM).astype(BF16)]

    @pl.when((chunk == 0) & (pl.program_id(1) == 0))
    def _():
        for vc in (vc1, vc4, vc16):
            for h in range(2):
                vc[h, :, :, PAIR:] = jnp.broadcast_to(
                    head_mask[h], vc.shape[1:3] + (PAIR,))
        row = lax.broadcasted_iota(jnp.int32, (ATTN_BLOCK, 2 * ATTN_BLOCK), 0)
        col = lax.broadcasted_iota(jnp.int32, (ATTN_BLOCK, 2 * ATTN_BLOCK), 1)
        row4 = ((row & (quarter - 1)) << 2) | (row >> 5)
        for kind, r in enumerate((row, row4)):
            band = (col >= r) & (col <= r + N_BACK)
            bias_ref[2 * kind] = jnp.where(band, 0.0, -jnp.inf)
            bias_ref[2 * kind + 1] = jnp.where(band & (col >= ATTN_BLOCK), 0.0, -jnp.inf)

    def put(kc, vc, cls, rows, k, v):
        kc[cls, rows] = k
        for h in range(2):
            vc[h, cls, rows, :PAIR] = v * head_mask[h]

    prev = slice(0, ATTN_BLOCK)
    put(kc1, vc1, 0, prev, k1p[...], v1p[...])
    put(kc1, vc1, 0, slice(ATTN_BLOCK, None), k1[...], v1[...])
    for r in range(4):
        put(kc4, vc4, r, prev, k4p[0, r], v4p[0, r])
        for i in range(4):
            rows = slice((i + 1) * ATTN_BLOCK, (i + 2) * ATTN_BLOCK)
            put(kc4, vc4, r, rows, k4[i, r], v4[i, r])
    for r in range(16):
        put(kc16, vc16, r, prev, k16p[0, r], v16p[0, r])
        put(kc16, vc16, r, slice(ATTN_BLOCK, None), k16[0, r], v16[0, r])

    max_ref[...] = jnp.full(max_ref.shape, -jnp.inf, F32)
    acc_ref[...] = jnp.zeros(acc_ref.shape, F32)
    den_ref[...] = jnp.zeros(den_ref.shape, F32)

    def window(j):
        return pl.ds(j * ATTN_BLOCK, 2 * ATTN_BLOCK)

    def dilation1(j):
        i, jj = divmod(j, 4)
        rows = slice(jj * quarter, (jj + 1) * quarter)
        q = jnp.concatenate([q4[i, r4, rows, :] for r4 in range(4)], axis=0)
        pieces = [(pl.ds(i * sub_rows + r4 * ATTN_BLOCK + jj * quarter, quarter),
                   slice(r4 * quarter, (r4 + 1) * quarter)) for r4 in range(4)]
        return (q, lambda: kc1[0, window(j), :], lambda h: vc1[h, 0, window(j), :],
                True if j > 0 else chunk > 0, 1, pieces)

    def dilation4(t):
        i, r4 = divmod(t, 4)
        pieces = [(pl.ds(i * sub_rows + r4 * ATTN_BLOCK, ATTN_BLOCK),
                   slice(0, ATTN_BLOCK))]
        return (q4[i, r4], lambda: kc4[r4, window(i), :],
                lambda h: vc4[h, r4, window(i), :],
                True if i > 0 else chunk > 0, 0, pieces)

    def dilation16(r):
        e, r4 = divmod(r, 4)
        pieces = [(pl.ds(i * sub_rows + r4 * ATTN_BLOCK + e, quarter, stride=4),
                   slice(i * quarter, (i + 1) * quarter)) for i in range(4)]
        return (q16[0, r], lambda: kc16[r], lambda h: vc16[h, r],
                chunk > 0, 0, pieces)

    makers = [dilation4] * nblk + [dilation16] * nblk + [dilation1] * nblk
    n_tiles = len(makers)

    def scores(t):
        q, keys, _, has_prev, kind, pieces = makers[t](t % nblk)
        no_prev = 0 if has_prev is True else jnp.where(has_prev, 0, 1)
        bias = bias_ref[2 * kind + no_prev]
        kk = keys()
        for h in range(2):
            s = lax.dot_general(q * head_mask[h], kk, (((1,), (1,)), ((), ())),
                                preferred_element_type=F32) + bias
            s_ref[t, h] = s
            m = jnp.broadcast_to(jnp.max(s, axis=-1, keepdims=True),
                                 (ATTN_BLOCK, PAIR))
            for canon, rows in pieces:
                max_ref[h, canon, :] = jnp.maximum(max_ref[h, canon, :], m[rows])

    def softmax(t):
        pieces = makers[t](t % nblk)[5]
        for h in range(2):
            m = jnp.concatenate([max_ref[h, canon, :] for canon, _ in pieces], axis=0)
            p = jnp.exp2(s_ref[t, h] - jnp.concatenate([m, m], axis=1))
            p_ref[t % nblk, h] = p.astype(BF16)

    def values(t):
        _, _, vals, _, _, pieces = makers[t](t % nblk)
        pv = (jnp.dot(p_ref[t % nblk, 0], vals(0), preferred_element_type=F32)
              + jnp.dot(p_ref[t % nblk, 1], vals(1), preferred_element_type=F32))
        for canon, rows in pieces:
            acc_ref[canon, :] += pv[rows, :PAIR]
            den_ref[canon, :] += pv[rows, PAIR:]

    def pipeline(stages):
        n_groups = n_tiles // ATTN_GROUP
        for i in range(n_groups + len(stages) - 1):
            for lag in reversed(range(len(stages))):
                if 0 <= i - lag < n_groups:
                    for j in range(ATTN_GROUP):
                        stages[lag]((i - lag) * ATTN_GROUP + j)

    pipeline([scores])
    pipeline([softmax, values])

    for i in range(ATTN_CHUNK // sub_rows):
        for r4 in range(4):
            canon = pl.ds(i * sub_rows + r4 * ATTN_BLOCK, ATTN_BLOCK)
            nat_ref[pl.ds(i * sub_rows + r4, ATTN_BLOCK, stride=4), :] = (
                acc_ref[canon, :] / den_ref[canon, :])
    o_ref[...] = nat_ref[...].astype(BF16)


def _attn_call(k1, v1, q4, k4, v4, q16, k16, v16):
    s = k1.shape[0]
    n_pairs = D_ATTN // PAIR
    blocks_per_chunk = ATTN_CHUNK // ATTN_BLOCK
    sub_per_chunk = ATTN_CHUNK // (4 * ATTN_BLOCK)

    nat = pl.BlockSpec((ATTN_CHUNK, PAIR), lambda c, p: (c, p))
    nat_prev = pl.BlockSpec(
        (ATTN_BLOCK, PAIR),
        lambda c, p: (jnp.maximum(c * blocks_per_chunk - 1, 0), p))
    c4 = pl.BlockSpec((sub_per_chunk, 4, ATTN_BLOCK, PAIR), lambda c, p: (c, 0, 0, p))
    c4_prev = pl.BlockSpec(
        (1, 4, ATTN_BLOCK, PAIR),
        lambda c, p: (jnp.maximum(c * sub_per_chunk - 1, 0), 0, 0, p))
    c16 = pl.BlockSpec((1, 16, ATTN_BLOCK, PAIR), lambda c, p: (c, 0, 0, p))
    c16_prev = pl.BlockSpec((1, 16, ATTN_BLOCK, PAIR),
                            lambda c, p: (jnp.maximum(c - 1, 0), 0, 0, p))
    kcat = lambda classes, blocks: pltpu.VMEM(
        (classes, (blocks + 1) * ATTN_BLOCK, PAIR), BF16)
    vcat = lambda classes, blocks: pltpu.VMEM(
        (2, classes, (blocks + 1) * ATTN_BLOCK, 2 * PAIR), BF16)
    rows = lambda: pltpu.VMEM((ATTN_CHUNK, PAIR), F32)
    return pl.pallas_call(
        _attn_kernel,
        out_shape=jax.ShapeDtypeStruct((s, D_ATTN), BF16),
        grid=(s // ATTN_CHUNK, n_pairs),
        in_specs=[nat] * 2 + [nat_prev] * 2 + [c4] * 3 + [c4_prev] * 2
                 + [c16] * 3 + [c16_prev] * 2,
        out_specs=nat,
        scratch_shapes=[
            kcat(1, 16), vcat(1, 16), kcat(4, 4), vcat(4, 4), kcat(16, 1), vcat(16, 1),
            pltpu.VMEM((4, ATTN_BLOCK, 2 * ATTN_BLOCK), F32),
            pltpu.VMEM((3 * blocks_per_chunk, 2, ATTN_BLOCK, 2 * ATTN_BLOCK), F32),
            pltpu.VMEM((blocks_per_chunk, 2, ATTN_BLOCK, 2 * ATTN_BLOCK), BF16),
            pltpu.VMEM((2, ATTN_CHUNK, PAIR), F32),
            rows(), rows(), rows()],
        compiler_params=pltpu.CompilerParams(
            dimension_semantics=("arbitrary", "arbitrary"),
            vmem_limit_bytes=56 * MIB),
        name="attn",
    )(k1, v1, k1, v1, q4, k4, v4, k4, v4, q16, k16, v16, k16, v16)


def _rglru_kernel(xr_ref, xp_ref, gr_ref, cw_ref, cb_ref, wx_ref, wa_ref,
                  bx_ref, ba_ref, lam_ref, o_ref,
                  xcat_ref, a_ref, b_ref, h_ref, nat_ref, carry_ref):
    step = pl.program_id(0)
    half = D_RNN // 2
    n_slabs = D_RNN // LANES
    halo = 8 * (CONV_WIDTH - 1)

    @pl.when(step == 0)
    def _():
        carry_ref[...] = jnp.zeros_like(carry_ref)

    def wide(ref, rows):
        return jnp.concatenate([ref[0, t, rows, :] for t in range(n_slabs)], axis=1)

    sub = lax.broadcasted_iota(jnp.int32, (8, D_RNN), 0)
    for d in range(1, CONV_WIDTH):
        cur = wide(xr_ref, pl.ds(8 * (SEG_LEN - d), 8))
        prv = jnp.where(step > 0, wide(xp_ref, pl.ds(8 * (CONV_WIDTH - d), 8)), 0.0)
        xcat_ref[pl.ds(halo - 8 * d, 8), :] = jnp.where(
            sub == 0, pltpu.roll(prv, 1, 0), pltpu.roll(cur, 1, 0))
    xcat_ref[halo:, :] = wide(xr_ref, slice(None))
    u = cb_ref[...]
    for k in range(CONV_WIDTH):
        d = CONV_WIDTH - 1 - k
        u = u + cw_ref[k:k + 1, :] * xcat_ref[pl.ds(halo - 8 * d, ROW_TILE), :]

    ub = u.astype(BF16)

    def gate(w_ref, b_ref):
        lo = jnp.dot(ub[:, :half], w_ref[0], preferred_element_type=F32)
        hi = jnp.dot(ub[:, half:], w_ref[1], preferred_element_type=F32)
        z = jnp.concatenate([lo, hi], axis=1) + b_ref[...]
        return 0.5 * jnp.tanh(0.5 * z) + 0.5

    gate_x = gate(wx_ref, bx_ref)
    gate_a = gate(wa_ref, ba_ref)
    neg_lam = -lam_ref[...]
    softplus = jnp.maximum(neg_lam, 0.0) + jnp.log1p(jnp.exp(-jnp.abs(neg_lam)))
    log_a = -LRU_C * gate_a * softplus
    a = jnp.exp(log_a)
    one_minus_a2 = -jnp.tanh(log_a) * (a * a + 1.0)
    root = jnp.where(one_minus_a2 > 0.0, one_minus_a2 * lax.rsqrt(one_minus_a2), 0.0)
    a_ref[...] = a
    b_ref[...] = u * gate_x * root

    def advance(j, carry):
        h_loc, prod = carry
        rows = pl.ds(pl.multiple_of(j * 8, 8), 8)
        a_j = a_ref[rows, :]
        h_loc = a_j * h_loc + b_ref[rows, :]
        prod = a_j * prod
        h_ref[rows, :] = h_loc
        a_ref[rows, :] = prod
        return h_loc, prod

    seg_end, seg_prod = lax.fori_loop(
        0, SEG_LEN, advance,
        (jnp.zeros((8, D_RNN), F32), jnp.ones((8, D_RNN), F32)), unroll=8)

    entering = [carry_ref[0:1, :]]
    for s in range(8):
        entering.append(seg_end[s:s + 1, :] + seg_prod[s:s + 1, :] * entering[s])
    carry_ref[0:1, :] = entering[8]
    state_in = jnp.tile(jnp.concatenate(entering[:8], axis=0), (SEG_LEN, 1))

    g = wide(gr_ref, slice(None))
    g_half = 0.5 * g
    inner = g * (math.sqrt(2.0 / math.pi)
                 + (0.044715 * math.sqrt(2.0 / math.pi)) * (g * g))
    gelu = g_half + g_half * jnp.tanh(inner)
    out = (h_ref[...] + a_ref[...] * state_in) * gelu

    for t in range(n_slabs):
        nat_ref[t] = out[:, t * LANES:(t + 1) * LANES]
        for s in range(8):
            o_ref[s * SEG_LEN:(s + 1) * SEG_LEN, t * LANES:(t + 1) * LANES] = (
                nat_ref[t, pl.ds(s, SEG_LEN, stride=8), :].astype(BF16))


def _rglru_call(xr, gr, conv_w, conv_b, wx, wa, b_x, b_a, lam):
    n_tiles, n_slabs = xr.shape[0], xr.shape[1]
    tile = pl.BlockSpec((1, n_slabs, ROW_TILE, LANES), lambda i: (i, 0, 0, 0))
    last_rows = 8 * CONV_WIDTH
    prev = pl.BlockSpec(
        (1, n_slabs, last_rows, LANES),
        lambda i: (jnp.maximum(i - 1, 0), 0, ROW_TILE // last_rows - 1, 0))
    vec = _resident((1, D_RNN))
    return pl.pallas_call(
        _rglru_kernel,
        out_shape=jax.ShapeDtypeStruct((n_tiles * ROW_TILE, D_RNN), BF16),
        grid=(n_tiles,),
        in_specs=[tile, prev, tile, _resident(conv_w.shape), vec,
                  _resident(wx.shape), _resident(wa.shape), vec, vec, vec],
        out_specs=pl.BlockSpec((ROW_TILE, D_RNN), lambda i: (i, 0)),
        scratch_shapes=[pltpu.VMEM((ROW_TILE + 8 * (CONV_WIDTH - 1), D_RNN), F32),
                        pltpu.VMEM((ROW_TILE, D_RNN), F32),
                        pltpu.VMEM((ROW_TILE, D_RNN), F32),
                        pltpu.VMEM((ROW_TILE, D_RNN), F32),
                        pltpu.VMEM((n_slabs, ROW_TILE, LANES), F32),
                        pltpu.VMEM((8, D_RNN), F32)],
        compiler_params=pltpu.CompilerParams(
            dimension_semantics=("arbitrary",), vmem_limit_bytes=32 * MIB),
        name="rglru",
    )(xr, xr, gr, conv_w, conv_b, wx, wa, b_x, b_a, lam)


def _outproj_ffn_kernel(x_ref, attn_ref, rec_ref, mod_ref, g_ref, wmix_ref,
                        win_ref, wout_ref, o_ref, acc_ref):
    y = jnp.dot(attn_ref[...], wmix_ref[:D_ATTN, :], preferred_element_type=F32)
    y = y + jnp.dot(rec_ref[...], wmix_ref[D_ATTN:, :], preferred_element_type=F32)
    x2 = _post_residual(x_ref[...], y, mod_ref, g_ref, 1, 1.0)
    o_ref[...] = _ffn(2, x2, mod_ref, g_ref, win_ref, wout_ref, acc_ref)


def _outproj_ffn_call(x, attn, rec, mod9, g, w_mix, w_in, w_out):
    row = pl.BlockSpec((ROW_TILE, D_MODEL), lambda i: (i, 0))
    half = pl.BlockSpec((ROW_TILE, D_ATTN), lambda i: (i, 0))
    return pl.pallas_call(
        _outproj_ffn_kernel,
        out_shape=jax.ShapeDtypeStruct(x.shape, F32),
        grid=(x.shape[0] // ROW_TILE,),
        in_specs=[row, half, half, _resident(mod9.shape), _resident(g.shape),
                  _resident(w_mix.shape), _resident(w_in.shape),
                  _resident(w_out.shape)],
        out_specs=row,
        scratch_shapes=[pltpu.VMEM((ROW_TILE, D_MODEL), F32)],
        compiler_params=pltpu.CompilerParams(
            dimension_semantics=("parallel",), vmem_limit_bytes=52 * MIB),
        name="outproj_ffn",
    )(x, attn, rec, mod9, g, w_mix, w_in, w_out)


def _blockdiag_halves(w):
    per_half = (D_RNN // 2) // LRU_BLOCK
    eye = jnp.eye(per_half, dtype=w.dtype)
    wh = w.reshape(2, per_half, LRU_BLOCK, LRU_BLOCK)
    dense = jnp.einsum('hgij,gk->hgikj', wh, eye)
    return dense.reshape(2, D_RNN // 2, D_RNN // 2).astype(BF16)


def kernel(x, c, w_ada, b_ada, norm_gain, ffn1_w_in, ffn1_w_out, mix_w_in, conv_w, conv_b,
           lru_w_x, lru_b_x, lru_w_a, lru_b_a, lru_lambda, mix_w_out, ffn2_w_in, ffn2_w_out):
    batch, seq, d = x.shape
    assert (batch, seq, d) == (1, SEQ, D_MODEL) and w_ada.shape[0] == 1
    xs = x.reshape(seq, d)
    g = norm_gain[0]

    mod = _mod_call(c.reshape(d, 1), w_ada[0], b_ada)
    mod9 = mod.reshape(9, d)

    x1, *qkv, xr, gr = _ffn_inproj_call(
        xs, mod9, g, ffn1_w_in[0].astype(BF16), ffn1_w_out[0].astype(BF16),
        mix_w_in[0].astype(BF16))
    attn = _attn_call(*qkv)
    rec = _rglru_call(xr, gr, conv_w[0], conv_b, _blockdiag_halves(lru_w_x[0]),
                      _blockdiag_halves(lru_w_a[0]), lru_b_x, lru_b_a, lru_lambda)
    x3 = _outproj_ffn_call(x1, attn, rec, mod9, g, mix_w_out[0].astype(BF16),
                           ffn2_w_in[0].astype(BF16), ffn2_w_out[0].astype(BF16))
    return x3.reshape(batch, seq, d)
```

```python
import functools
import math

import jax
import jax.numpy as jnp
from jax import lax
from jax.experimental import pallas as pl
from jax.experimental.pallas import tpu as pltpu

F32 = jnp.float32
BF16 = jnp.bfloat16

D_MODEL = 1024
SEQ = 16384
HEAD_DIM = 64
D_ATTN = 512
D_RNN = 512
LRU_BLOCK = 64
N_BACK = 128
ATTN_BLOCK = 128
CONV_WIDTH = 4
LRU_C = 8.0
D_FF = 2816
FFN_RES_WEIGHT = 0.5
RMS_EPS = 1e-6

MIB = 1024 * 1024
ROW_TILE = 512
SEG_LEN = ROW_TILE // 8
FF_CHUNK = 256
N_FF_CHUNKS = D_FF // FF_CHUNK
ATTN_CHUNK = 2048
LANES = 128
BF16_SUBLANES = 16
WEIGHT_STAGE_BYTES = MIB
PAIR = 2 * HEAD_DIM
MOD_TILE = 1152
ATTN_GROUP = 4


def _resident(shape):
    return pl.BlockSpec(shape, lambda *_: (0,) * len(shape),
                        pipeline_mode=pl.Buffered(1))


class _ResidentWeight:
    def __init__(self, w):
        rows, cols = w.shape
        chunk = WEIGHT_STAGE_BYTES // (cols * 4)
        chunk = max(BF16_SUBLANES, chunk // BF16_SUBLANES * BF16_SUBLANES)
        while rows % chunk:
            chunk -= BF16_SUBLANES
        self.shape, self.chunk = (rows, cols), chunk
        self.in_spec = pl.BlockSpec(memory_space=pl.ANY)
        self.scratch = [pltpu.VMEM((rows, cols), BF16),
                        pltpu.VMEM((2, chunk, cols), F32),
                        pltpu.SemaphoreType.DMA((2,))]

    def load(self, hbm_ref, dst_ref, stage_ref, sem):
        chunk = self.chunk
        n_chunks = self.shape[0] // chunk

        def copy(c):
            return pltpu.make_async_copy(hbm_ref.at[pl.ds(c * chunk, chunk)],
                                         stage_ref.at[c % 2], sem.at[c % 2])

        copy(0).start()
        for c in range(n_chunks):
            if c + 1 < n_chunks:
                copy(c + 1).start()
            copy(c).wait()
            dst_ref[c * chunk:(c + 1) * chunk, :] = stage_ref[c % 2].astype(BF16)


def _load_weights_once(weights, hbm_refs, scratch_refs):
    per_weight = [scratch_refs[3 * i:3 * i + 3] for i in range(len(weights))]

    @pl.when(pl.program_id(0) == 0)
    def _():
        for w, hbm, (dst, stage, sem) in zip(weights, hbm_refs, per_weight):
            w.load(hbm, dst, stage, sem)

    return [dst for dst, _, _ in per_weight]


def _rms(v):
    return v * lax.rsqrt(jnp.mean(v * v, axis=-1, keepdims=True) + RMS_EPS)


def _pre(x, mod_ref, g_ref, sub):
    shift = mod_ref[3 * sub:3 * sub + 1, :]
    scale = mod_ref[3 * sub + 1:3 * sub + 2, :]
    gain = g_ref[2 * sub:2 * sub + 1, :] * (1.0 + scale)
    return _rms(x) * gain + shift


def _post_residual(x, y, mod_ref, g_ref, sub, weight):
    gate = mod_ref[3 * sub + 2:3 * sub + 3, :]
    gain = weight * (1.0 + gate) * g_ref[2 * sub + 1:2 * sub + 2, :]
    return x + _rms(y) * gain


def _mod_kernel(c_ref, w_ref, b_ref, o_ref):
    c = c_ref[...]
    s = c * jax.nn.sigmoid(c)
    o_ref[...] = jnp.sum(s * w_ref[...], axis=0, keepdims=True) + b_ref[...]


def _mod_call(c_col, w_ada, b_ada):
    n = w_ada.shape[1]
    return pl.pallas_call(
        _mod_kernel,
        out_shape=jax.ShapeDtypeStruct((1, n), F32),
        grid=(n // MOD_TILE,),
        in_specs=[
            pl.BlockSpec((D_MODEL, 1), lambda j: (0, 0)),
            pl.BlockSpec((D_MODEL, MOD_TILE), lambda j: (0, j)),
            pl.BlockSpec((1, MOD_TILE), lambda j: (0, j)),
        ],
        out_specs=pl.BlockSpec((1, MOD_TILE), lambda j: (0, j)),
        compiler_params=pltpu.CompilerParams(
            dimension_semantics=("arbitrary",), vmem_limit_bytes=24 * MIB),
        name="mod",
    )(c_col, w_ada, b_ada)


def _ffn(sub, x, mod_ref, g_ref, win_ref, wout_ref, acc_ref):
    h = _pre(x, mod_ref, g_ref, sub).astype(BF16)

    def gate_up(c):
        lo = c * FF_CHUNK
        return (jnp.dot(h, win_ref[:, lo:lo + FF_CHUNK], preferred_element_type=F32),
                jnp.dot(h, win_ref[:, D_FF + lo:D_FF + lo + FF_CHUNK],
                        preferred_element_type=F32))

    hg, hu = gate_up(0)
    for c in range(N_FF_CHUNKS):
        a = (hg * jax.nn.sigmoid(hg) * hu).astype(BF16)
        if c + 1 < N_FF_CHUNKS:
            hg, hu = gate_up(c + 1)
        part = jnp.dot(a, wout_ref[c * FF_CHUNK:(c + 1) * FF_CHUNK, :],
                       preferred_element_type=F32)
        if c == 0:
            acc_ref[...] = part
        else:
            acc_ref[...] += part
    return _post_residual(x, acc_ref[...], mod_ref, g_ref, sub, FFN_RES_WEIGHT)


def _ffn_inproj_kernel(weights, x_ref, mod_ref, g_ref, *refs):
    (x1_ref, k1, v1, q4, k4, v4, q16, k16, v16, xr_ref, gr_ref,
     acc_ref, z_ref, z4_ref) = refs[3:17]
    win_ref, wout_ref, wmix_ref = _load_weights_once(weights, refs[:3], refs[17:])
    x1 = _ffn(0, x_ref[...], mod_ref, g_ref, win_ref, wout_ref, acc_ref)
    x1_ref[...] = x1
    h = _pre(x1, mod_ref, g_ref, 1).astype(BF16)

    def proj(j, width):
        return jnp.dot(h, wmix_ref[:, j:j + width], preferred_element_type=F32)

    outs = ((None, q4, q16), (k1, k4, k16), (v1, v4, v16))
    for n, (o1, o4, o16) in enumerate(outs):
        z = proj(n * D_ATTN, D_ATTN)
        if n == 0:
            z = z * (HEAD_DIM ** -0.5 * math.log2(math.e))
        for t in range(D_ATTN // LANES):
            cols = slice(t * LANES, (t + 1) * LANES)
            if o1 is not None:
                o1[:, cols] = z[:, cols].astype(BF16)
            z_ref[n, t] = z[:, cols]
            for r4 in range(4):
                z4 = z_ref[n, t, pl.ds(r4, ROW_TILE // 4, stride=4), :]
                o4[0, r4, :, cols] = z4.astype(BF16)
                z4_ref[n, t, r4] = z4
                for e in range(4):
                    z16 = z4_ref[n, t, r4, pl.ds(e, ROW_TILE // 16, stride=4), :]
                    o16[0, r4 + 4 * e, :, cols] = z16.astype(BF16)

    for n, dst in enumerate((xr_ref, gr_ref)):
        z = proj(3 * D_ATTN + n * D_RNN, D_RNN)
        for t in range(D_RNN // LANES):
            for s in range(8):
                dst[0, t, pl.ds(s, SEG_LEN, stride=8), :] = z[
                    s * SEG_LEN:(s + 1) * SEG_LEN, t * LANES:(t + 1) * LANES]


def _ffn_inproj_call(x, mod9, g, w_in, w_out, w_mix):
    s = x.shape[0]
    row = pl.BlockSpec((ROW_TILE, D_MODEL), lambda i: (i, 0))
    nat = pl.BlockSpec((ROW_TILE, D_ATTN), lambda i: (i, 0))
    c4 = pl.BlockSpec((1, 4, ROW_TILE // 4, D_ATTN), lambda i: (i, 0, 0, 0))
    per16 = ATTN_CHUNK // ROW_TILE
    c16 = pl.BlockSpec((1, 16, ROW_TILE // 16, D_ATTN),
                       lambda i: (i // per16, 0, i % per16, 0))
    s_nat = jax.ShapeDtypeStruct((s, D_ATTN), BF16)
    s_c4 = jax.ShapeDtypeStruct((s // ROW_TILE, 4, ROW_TILE // 4, D_ATTN), BF16)
    s_c16 = jax.ShapeDtypeStruct((s // ATTN_CHUNK, 16, ATTN_BLOCK, D_ATTN), BF16)
    seg = pl.BlockSpec((1, D_RNN // LANES, ROW_TILE, LANES), lambda i: (i, 0, 0, 0))
    s_seg = jax.ShapeDtypeStruct((s // ROW_TILE, D_RNN // LANES, ROW_TILE, LANES), F32)
    weights = [_ResidentWeight(w) for w in (w_in, w_out, w_mix)]
    return pl.pallas_call(
        functools.partial(_ffn_inproj_kernel, weights),
        out_shape=([jax.ShapeDtypeStruct(x.shape, F32)]
                   + [s_nat] * 2 + [s_c4] * 3 + [s_c16] * 3 + [s_seg] * 2),
        grid=(s // ROW_TILE,),
        in_specs=[row, _resident(mod9.shape), _resident(g.shape)]
                 + [w.in_spec for w in weights],
        out_specs=[row] + [nat] * 2 + [c4] * 3 + [c16] * 3 + [seg] * 2,
        scratch_shapes=[pltpu.VMEM((ROW_TILE, D_MODEL), F32),
                        pltpu.VMEM((3, D_ATTN // LANES, ROW_TILE, LANES), F32),
                        pltpu.VMEM((3, D_ATTN // LANES, 4, ROW_TILE // 4, LANES), F32)]
                       + [s for w in weights for s in w.scratch],
        compiler_params=pltpu.CompilerParams(
            dimension_semantics=("arbitrary",), vmem_limit_bytes=58 * MIB),
        name="ffn_inproj",
    )(x, mod9, g, w_in, w_out, w_mix)


def _attn_kernel(k1, v1, k1p, v1p, q4, k4, v4, k4p, v4p,
                 q16, k16, v16, k16p, v16p, o_ref,
                 kc1, vc1, kc4, vc4, kc16, vc16,
                 bias_ref, s_ref, p_ref, max_ref, acc_ref, den_ref, nat_ref):
    chunk = pl.program_id(0)
    nblk = ATTN_CHUNK // ATTN_BLOCK
    sub_rows = 4 * ATTN_BLOCK
    quarter = ATTN_BLOCK // 4
    lane = lax.broadcasted_iota(jnp.int32, (1, PAIR), 1)
    head_mask = [(lane < HEAD_DIM).astype(BF16), (lane >= HEAD_DIM).astype(BF16)]

    @pl.when((chunk == 0) & (pl.program_id(1) == 0))
    def _():
        for vc in (vc1, vc4, vc16):
            for h in range(2):
                vc[h, :, :, PAIR:] = jnp.broadcast_to(
                    head_mask[h], vc.shape[1:3] + (PAIR,))
        row = lax.broadcasted_iota(jnp.int32, (ATTN_BLOCK, 2 * ATTN_BLOCK), 0)
        col = lax.broadcasted_iota(jnp.int32, (ATTN_BLOCK, 2 * ATTN_BLOCK), 1)
        row4 = ((row & (quarter - 1)) << 2) | (row >> 5)
        for kind, r in enumerate((row, row4)):
            band = (col >= r) & (col <= r + N_BACK)
            bias_ref[2 * kind] = jnp.where(band, 0.0, -jnp.inf)
            bias_ref[2 * kind + 1] = jnp.where(band & (col >= ATTN_BLOCK), 0.0, -jnp.inf)

    def put(kc, vc, cls, rows, k, v):
        kc[cls, rows] = k
        for h in range(2):
            vc[h, cls, rows, :PAIR] = v * head_mask[h]

    prev = slice(0, ATTN_BLOCK)
    put(kc1, vc1, 0, prev, k1p[...], v1p[...])
    put(kc1, vc1, 0, slice(ATTN_BLOCK, None), k1[...], v1[...])
    for r in range(4):
        put(kc4, vc4, r, prev, k4p[0, r], v4p[0, r])
        for i in range(4):
            rows = slice((i + 1) * ATTN_BLOCK, (i + 2) * ATTN_BLOCK)
            put(kc4, vc4, r, rows, k4[i, r], v4[i, r])
    for r in range(16):
        put(kc16, vc16, r, prev, k16p[0, r], v16p[0, r])
        put(kc16, vc16, r, slice(ATTN_BLOCK, None), k16[0, r], v16[0, r])

    max_ref[...] = jnp.full(max_ref.shape, -jnp.inf, F32)
    acc_ref[...] = jnp.zeros(acc_ref.shape, F32)
    den_ref[...] = jnp.zeros(den_ref.shape, F32)

    def window(j):
        return pl.ds(j * ATTN_BLOCK, 2 * ATTN_BLOCK)

    def dilation1(j):
        i, jj = divmod(j, 4)
        rows = slice(jj * quarter, (jj + 1) * quarter)
        q = jnp.concatenate([q4[i, r4, rows, :] for r4 in range(4)], axis=0)
        pieces = [(pl.ds(i * sub_rows + r4 * ATTN_BLOCK + jj * quarter, quarter),
                   slice(r4 * quarter, (r4 + 1) * quarter)) for r4 in range(4)]
        return (q, lambda: kc1[0, window(j), :], lambda h: vc1[h, 0, window(j), :],
                True if j > 0 else chunk > 0, 1, pieces)

    def dilation4(t):
        i, r4 = divmod(t, 4)
        pieces = [(pl.ds(i * sub_rows + r4 * ATTN_BLOCK, ATTN_BLOCK),
                   slice(0, ATTN_BLOCK))]
        return (q4[i, r4], lambda: kc4[r4, window(i), :],
                lambda h: vc4[h, r4, window(i), :],
                True if i > 0 else chunk > 0, 0, pieces)

    def dilation16(r):
        e, r4 = divmod(r, 4)
        pieces = [(pl.ds(i * sub_rows + r4 * ATTN_BLOCK + e, quarter, stride=4),
                   slice(i * quarter, (i + 1) * quarter)) for i in range(4)]
        return (q16[0, r], lambda: kc16[r], lambda h: vc16[h, r],
                chunk > 0, 0, pieces)

    makers = [dilation4] * nblk + [dilation16] * nblk + [dilation1] * nblk
    n_tiles = len(makers)

    def scores(t):
        q, keys, _, has_prev, kind, pieces = makers[t](t % nblk)
        no_prev = 0 if has_prev is True else jnp.where(has_prev, 0, 1)
        bias = bias_ref[2 * kind + no_prev]
        kk = keys()
        for h in range(2):
            s = lax.dot_general(q * head_mask[h], kk, (((1,), (1,)), ((), ())),
                                preferred_element_type=F32) + bias
            s_ref[t, h] = s
            m = jnp.broadcast_to(jnp.max(s, axis=-1, keepdims=True),
                                 (ATTN_BLOCK, PAIR))
            for canon, rows in pieces:
                max_ref[h, canon, :] = jnp.maximum(max_ref[h, canon, :], m[rows])

    def softmax(t):
        pieces = makers[t](t % nblk)[5]
        for h in range(2):
            m = jnp.concatenate([max_ref[h, canon, :] for canon, _ in pieces], axis=0)
            p = jnp.exp2(s_ref[t, h] - jnp.concatenate([m, m], axis=1))
            p_ref[t % nblk, h] = p.astype(BF16)

    def values(t):
        _, _, vals, _, _, pieces = makers[t](t % nblk)
        pv = (jnp.dot(p_ref[t % nblk, 0], vals(0), preferred_element_type=F32)
              + jnp.dot(p_ref[t % nblk, 1], vals(1), preferred_element_type=F32))
        for canon, rows in pieces:
            acc_ref[canon, :] += pv[rows, :PAIR]
            den_ref[canon, :] += pv[rows, PAIR:]

    def pipeline(stages):
        n_groups = n_tiles // ATTN_GROUP
        for i in range(n_groups + len(stages) - 1):
            for lag in reversed(range(len(stages))):
                if 0 <= i - lag < n_groups:
                    for j in range(ATTN_GROUP):
                        stages[lag]((i - lag) * ATTN_GROUP + j)

    pipeline([scores])
    pipeline([softmax, values])

    for i in range(ATTN_CHUNK // sub_rows):
        for r4 in range(4):
            canon = pl.ds(i * sub_rows + r4 * ATTN_BLOCK, ATTN_BLOCK)
            nat_ref[pl.ds(i * sub_rows + r4, ATTN_BLOCK, stride=4), :] = (
                acc_ref[canon, :] / den_ref[canon, :])
    o_ref[...] = nat_ref[...].astype(BF16)


def _attn_call(k1, v1, q4, k4, v4, q16, k16, v16):
    s = k1.shape[0]
    n_pairs = D_ATTN // PAIR
    blocks_per_chunk = ATTN_CHUNK // ATTN_BLOCK
    sub_per_chunk = ATTN_CHUNK // (4 * ATTN_BLOCK)

    nat = pl.BlockSpec((ATTN_CHUNK, PAIR), lambda c, p: (c, p))
    nat_prev = pl.BlockSpec(
        (ATTN_BLOCK, PAIR),
        lambda c, p: (jnp.maximum(c * blocks_per_chunk - 1, 0), p))
    c4 = pl.BlockSpec((sub_per_chunk, 4, ATTN_BLOCK, PAIR), lambda c, p: (c, 0, 0, p))
    c4_prev = pl.BlockSpec(
        (1, 4, ATTN_BLOCK, PAIR),
        lambda c, p: (jnp.maximum(c * sub_per_chunk - 1, 0), 0, 0, p))
    c16 = pl.BlockSpec((1, 16, ATTN_BLOCK, PAIR), lambda c, p: (c, 0, 0, p))
    c16_prev = pl.BlockSpec((1, 16, ATTN_BLOCK, PAIR),
                            lambda c, p: (jnp.maximum(c - 1, 0), 0, 0, p))
    kcat = lambda classes, blocks: pltpu.VMEM(
        (classes, (blocks + 1) * ATTN_BLOCK, PAIR), BF16)
    vcat = lambda classes, blocks: pltpu.VMEM(
        (2, classes, (blocks + 1) * ATTN_BLOCK, 2 * PAIR), BF16)
    rows = lambda: pltpu.VMEM((ATTN_CHUNK, PAIR), F32)
    return pl.pallas_call(
        _attn_kernel,
        out_shape=jax.ShapeDtypeStruct((s, D_ATTN), BF16),
        grid=(s // ATTN_CHUNK, n_pairs),
        in_specs=[nat] * 2 + [nat_prev] * 2 + [c4] * 3 + [c4_prev] * 2
                 + [c16] * 3 + [c16_prev] * 2,
        out_specs=nat,
        scratch_shapes=[
            kcat(1, 16), vcat(1, 16), kcat(4, 4), vcat(4, 4), kcat(16, 1), vcat(16, 1),
            pltpu.VMEM((4, ATTN_BLOCK, 2 * ATTN_BLOCK), F32),
            pltpu.VMEM((3 * blocks_per_chunk, 2, ATTN_BLOCK, 2 * ATTN_BLOCK), F32),
            pltpu.VMEM((blocks_per_chunk, 2, ATTN_BLOCK, 2 * ATTN_BLOCK), BF16),
            pltpu.VMEM((2, ATTN_CHUNK, PAIR), F32),
            rows(), rows(), rows()],
        compiler_params=pltpu.CompilerParams(
            dimension_semantics=("arbitrary", "arbitrary"),
            vmem_limit_bytes=56 * MIB),
        name="attn",
    )(k1, v1, k1, v1, q4, k4, v4, k4, v4, q16, k16, v16, k16, v16)


def _rglru_kernel(xr_ref, xp_ref, gr_ref, cw_ref, cb_ref, wx_ref, wa_ref,
                  bx_ref, ba_ref, lam_ref, o_ref,
                  xcat_ref, a_ref, b_ref, h_ref, nat_ref, carry_ref):
    step = pl.program_id(0)
    half = D_RNN // 2
    n_slabs = D_RNN // LANES
    halo = 8 * (CONV_WIDTH - 1)

    @pl.when(step == 0)
    def _():
        carry_ref[...] = jnp.zeros_like(carry_ref)

    def wide(ref, rows):
        return jnp.concatenate([ref[0, t, rows, :] for t in range(n_slabs)], axis=1)

    sub = lax.broadcasted_iota(jnp.int32, (8, D_RNN), 0)
    for d in range(1, CONV_WIDTH):
        cur = wide(xr_ref, pl.ds(8 * (SEG_LEN - d), 8))
        prv = jnp.where(step > 0, wide(xp_ref, pl.ds(8 * (CONV_WIDTH - d), 8)), 0.0)
        xcat_ref[pl.ds(halo - 8 * d, 8), :] = jnp.where(
            sub == 0, pltpu.roll(prv, 1, 0), pltpu.roll(cur, 1, 0))
    xcat_ref[halo:, :] = wide(xr_ref, slice(None))
    u = cb_ref[...]
    for k in range(CONV_WIDTH):
        d = CONV_WIDTH - 1 - k
        u = u + cw_ref[k:k + 1, :] * xcat_ref[pl.ds(halo - 8 * d, ROW_TILE), :]

    ub = u.astype(BF16)

    def gate(w_ref, b_ref):
        lo = jnp.dot(ub[:, :half], w_ref[0], preferred_element_type=F32)
        hi = jnp.dot(ub[:, half:], w_ref[1], preferred_element_type=F32)
        z = jnp.concatenate([lo, hi], axis=1) + b_ref[...]
        return 0.5 * jnp.tanh(0.5 * z) + 0.5

    gate_x = gate(wx_ref, bx_ref)
    gate_a = gate(wa_ref, ba_ref)
    neg_lam = -lam_ref[...]
    softplus = jnp.maximum(neg_lam, 0.0) + jnp.log1p(jnp.exp(-jnp.abs(neg_lam)))
    log_a = -LRU_C * gate_a * softplus
    a = jnp.exp(log_a)
    one_minus_a2 = -jnp.tanh(log_a) * (a * a + 1.0)
    root = jnp.where(one_minus_a2 > 0.0, one_minus_a2 * lax.rsqrt(one_minus_a2), 0.0)
    a_ref[...] = a
    b_ref[...] = u * gate_x * root

    def advance(j, carry):
        h_loc, prod = carry
        rows = pl.ds(pl.multiple_of(j * 8, 8), 8)
        a_j = a_ref[rows, :]
        h_loc = a_j * h_loc + b_ref[rows, :]
        prod = a_j * prod
        h_ref[rows, :] = h_loc
        a_ref[rows, :] = prod
        return h_loc, prod

    seg_end, seg_prod = lax.fori_loop(
        0, SEG_LEN, advance,
        (jnp.zeros((8, D_RNN), F32), jnp.ones((8, D_RNN), F32)), unroll=8)

    entering = [carry_ref[0:1, :]]
    for s in range(8):
        entering.append(seg_end[s:s + 1, :] + seg_prod[s:s + 1, :] * entering[s])
    carry_ref[0:1, :] = entering[8]
    state_in = jnp.tile(jnp.concatenate(entering[:8], axis=0), (SEG_LEN, 1))

    g = wide(gr_ref, slice(None))
    g_half = 0.5 * g
    inner = g * (math.sqrt(2.0 / math.pi)
                 + (0.044715 * math.sqrt(2.0 / math.pi)) * (g * g))
    gelu = g_half + g_half * jnp.tanh(inner)
    out = (h_ref[...] + a_ref[...] * state_in) * gelu

    for t in range(n_slabs):
        nat_ref[t] = out[:, t * LANES:(t + 1) * LANES]
        for s in range(8):
            o_ref[s * SEG_LEN:(s + 1) * SEG_LEN, t * LANES:(t + 1) * LANES] = (
                nat_ref[t, pl.ds(s, SEG_LEN, stride=8), :].astype(BF16))


def _rglru_call(xr, gr, conv_w, conv_b, wx, wa, b_x, b_a, lam):
    n_tiles, n_slabs = xr.shape[0], xr.shape[1]
    tile = pl.BlockSpec((1, n_slabs, ROW_TILE, LANES), lambda i: (i, 0, 0, 0))
    last_rows = 8 * CONV_WIDTH
    prev = pl.BlockSpec(
        (1, n_slabs, last_rows, LANES),
        lambda i: (jnp.maximum(i - 1, 0), 0, ROW_TILE // last_rows - 1, 0))
    vec = _resident((1, D_RNN))
    return pl.pallas_call(
        _rglru_kernel,
        out_shape=jax.ShapeDtypeStruct((n_tiles * ROW_TILE, D_RNN), BF16),
        grid=(n_tiles,),
        in_specs=[tile, prev, tile, _resident(conv_w.shape), vec,
                  _resident(wx.shape), _resident(wa.shape), vec, vec, vec],
        out_specs=pl.BlockSpec((ROW_TILE, D_RNN), lambda i: (i, 0)),
        scratch_shapes=[pltpu.VMEM((ROW_TILE + 8 * (CONV_WIDTH - 1), D_RNN), F32),
                        pltpu.VMEM((ROW_TILE, D_RNN), F32),
                        pltpu.VMEM((ROW_TILE, D_RNN), F32),
                        pltpu.VMEM((ROW_TILE, D_RNN), F32),
                        pltpu.VMEM((n_slabs, ROW_TILE, LANES), F32),
                        pltpu.VMEM((8, D_RNN), F32)],
        compiler_params=pltpu.CompilerParams(
            dimension_semantics=("arbitrary",), vmem_limit_bytes=32 * MIB),
        name="rglru",
    )(xr, xr, gr, conv_w, conv_b, wx, wa, b_x, b_a, lam)


def _outproj_ffn_kernel(weights, x_ref, attn_ref, rec_ref, mod_ref, g_ref, *refs):
    o_ref, acc_ref = refs[3:5]
    wmix_ref, win_ref, wout_ref = _load_weights_once(weights, refs[:3], refs[5:])
    y = jnp.dot(attn_ref[...], wmix_ref[:D_ATTN, :], preferred_element_type=F32)
    y = y + jnp.dot(rec_ref[...], wmix_ref[D_ATTN:, :], preferred_element_type=F32)
    x2 = _post_residual(x_ref[...], y, mod_ref, g_ref, 1, 1.0)
    o_ref[...] = _ffn(2, x2, mod_ref, g_ref, win_ref, wout_ref, acc_ref)


def _outproj_ffn_call(x, attn, rec, mod9, g, w_mix, w_in, w_out):
    row = pl.BlockSpec((ROW_TILE, D_MODEL), lambda i: (i, 0))
    half = pl.BlockSpec((ROW_TILE, D_ATTN), lambda i: (i, 0))
    weights = [_ResidentWeight(w) for w in (w_mix, w_in, w_out)]
    return pl.pallas_call(
        functools.partial(_outproj_ffn_kernel, weights),
        out_shape=jax.ShapeDtypeStruct(x.shape, F32),
        grid=(x.shape[0] // ROW_TILE,),
        in_specs=[row, half, half, _resident(mod9.shape), _resident(g.shape)]
                 + [w.in_spec for w in weights],
        out_specs=row,
        scratch_shapes=[pltpu.VMEM((ROW_TILE, D_MODEL), F32)]
                       + [s for w in weights for s in w.scratch],
        compiler_params=pltpu.CompilerParams(
            dimension_semantics=("arbitrary",), vmem_limit_bytes=52 * MIB),
        name="outproj_ffn",
    )(x, attn, rec, mod9, g, w_mix, w_in, w_out)


def _blockdiag_halves(w):
    per_half = (D_RNN // 2) // LRU_BLOCK
    eye = jnp.eye(per_half, dtype=w.dtype)
    wh = w.reshape(2, per_half, LRU_BLOCK, LRU_BLOCK)
    dense = jnp.einsum('hgij,gk->hgikj', wh, eye)
    return dense.reshape(2, D_RNN // 2, D_RNN // 2).astype(BF16)


def kernel(x, c, w_ada, b_ada, norm_gain, ffn1_w_in, ffn1_w_out, mix_w_in, conv_w, conv_b,
           lru_w_x, lru_b_x, lru_w_a, lru_b_a, lru_lambda, mix_w_out, ffn2_w_in, ffn2_w_out):
    batch, seq, d = x.shape
    assert (batch, seq, d) == (1, SEQ, D_MODEL) and w_ada.shape[0] == 1
    xs = x.reshape(seq, d)
    g = norm_gain[0]

    mod = _mod_call(c.reshape(d, 1), w_ada[0], b_ada)
    mod9 = mod.reshape(9, d)

    x1, *qkv, xr, gr = _ffn_inproj_call(
        xs, mod9, g, ffn1_w_in[0], ffn1_w_out[0], mix_w_in[0])
    attn = _attn_call(*qkv)
    rec = _rglru_call(xr, gr, conv_w[0], conv_b, _blockdiag_halves(lru_w_x[0]),
                      _blockdiag_halves(lru_w_a[0]), lru_b_x, lru_b_a, lru_lambda)
    x3 = _outproj_ffn_call(x1, attn, rec, mod9, g, mix_w_out[0],
                           ffn2_w_in[0], ffn2_w_out[0])
    return x3.reshape(batch, seq, d)
```

```python
import math

import jax
import jax.numpy as jnp
from jax import lax
from jax.experimental import pallas as pl
from jax.experimental.pallas import tpu as pltpu

F32 = jnp.float32
BF16 = jnp.bfloat16

D_MODEL = 1024
SEQ = 16384
HEAD_DIM = 64
D_ATTN = 512
D_RNN = 512
LRU_BLOCK = 64
N_BACK = 128
ATTN_BLOCK = 128
CONV_WIDTH = 4
LRU_C = 8.0
D_FF = 2816
FFN_RES_WEIGHT = 0.5
RMS_EPS = 1e-6

MIB = 1024 * 1024
ROW_TILE = 512
SEG_LEN = ROW_TILE // 8
FF_CHUNK = 256
N_FF_CHUNKS = D_FF // FF_CHUNK
ATTN_CHUNK = 2048
LANES = 128
STAGE_ROWS, STAGE_COLS = 256, 512
STAGE_SLOTS = 8
PAIR = 2 * HEAD_DIM
MOD_TILE = 1152
ATTN_GROUP = 4


def _resident(shape):
    return pl.BlockSpec(shape, lambda *_: (0,) * len(shape),
                        pipeline_mode=pl.Buffered(1))


def _weight_scratch(weights):
    for w in weights:
        assert w.shape[0] % STAGE_ROWS == 0 and w.shape[1] % STAGE_COLS == 0, w.shape
    return ([pltpu.VMEM(w.shape, BF16) for w in weights]
            + [pltpu.VMEM((STAGE_SLOTS, STAGE_ROWS, STAGE_COLS), F32),
               pltpu.SemaphoreType.DMA((STAGE_SLOTS,))])


def _load_weights_once(hbm_refs, scratch_refs):
    *dst_refs, stage_ref, sem = scratch_refs
    blocks = [(hbm, dst, r, c)
              for hbm, dst in zip(hbm_refs, dst_refs)
              for r in range(0, dst.shape[0], STAGE_ROWS)
              for c in range(0, dst.shape[1], STAGE_COLS)]

    def copy(i):
        hbm, _, r, c = blocks[i]
        slot = i % STAGE_SLOTS
        return pltpu.make_async_copy(
            hbm.at[pl.ds(r, STAGE_ROWS), pl.ds(c, STAGE_COLS)],
            stage_ref.at[slot], sem.at[slot])

    @pl.when(pl.program_id(0) == 0)
    def _():
        for i in range(min(STAGE_SLOTS, len(blocks))):
            copy(i).start()
        for i, (_, dst, r, c) in enumerate(blocks):
            copy(i).wait()
            dst[r:r + STAGE_ROWS, c:c + STAGE_COLS] = (
                stage_ref[i % STAGE_SLOTS].astype(BF16))
            if i + STAGE_SLOTS < len(blocks):
                copy(i + STAGE_SLOTS).start()

    return dst_refs


def _rms(v):
    return v * lax.rsqrt(jnp.mean(v * v, axis=-1, keepdims=True) + RMS_EPS)


def _pre(x, mod_ref, g_ref, sub):
    shift = mod_ref[3 * sub:3 * sub + 1, :]
    scale = mod_ref[3 * sub + 1:3 * sub + 2, :]
    gain = g_ref[2 * sub:2 * sub + 1, :] * (1.0 + scale)
    return _rms(x) * gain + shift


def _post_residual(x, y, mod_ref, g_ref, sub, weight):
    gate = mod_ref[3 * sub + 2:3 * sub + 3, :]
    gain = weight * (1.0 + gate) * g_ref[2 * sub + 1:2 * sub + 2, :]
    return x + _rms(y) * gain


def _mod_kernel(c_ref, w_ref, b_ref, o_ref):
    c = c_ref[...]
    s = c * jax.nn.sigmoid(c)
    o_ref[...] = jnp.sum(s * w_ref[...], axis=0, keepdims=True) + b_ref[...]


def _mod_call(c_col, w_ada, b_ada):
    n = w_ada.shape[1]
    return pl.pallas_call(
        _mod_kernel,
        out_shape=jax.ShapeDtypeStruct((1, n), F32),
        grid=(n // MOD_TILE,),
        in_specs=[
            pl.BlockSpec((D_MODEL, 1), lambda j: (0, 0)),
            pl.BlockSpec((D_MODEL, MOD_TILE), lambda j: (0, j)),
            pl.BlockSpec((1, MOD_TILE), lambda j: (0, j)),
        ],
        out_specs=pl.BlockSpec((1, MOD_TILE), lambda j: (0, j)),
        compiler_params=pltpu.CompilerParams(
            dimension_semantics=("arbitrary",), vmem_limit_bytes=24 * MIB),
        name="mod",
    )(c_col, w_ada, b_ada)


def _ffn(sub, x, mod_ref, g_ref, win_ref, wout_ref, acc_ref):
    h = _pre(x, mod_ref, g_ref, sub).astype(BF16)

    def gate_up(c):
        lo = c * FF_CHUNK
        return (jnp.dot(h, win_ref[:, lo:lo + FF_CHUNK], preferred_element_type=F32),
                jnp.dot(h, win_ref[:, D_FF + lo:D_FF + lo + FF_CHUNK],
                        preferred_element_type=F32))

    hg, hu = gate_up(0)
    for c in range(N_FF_CHUNKS):
        a = (hg * jax.nn.sigmoid(hg) * hu).astype(BF16)
        if c + 1 < N_FF_CHUNKS:
            hg, hu = gate_up(c + 1)
        part = jnp.dot(a, wout_ref[c * FF_CHUNK:(c + 1) * FF_CHUNK, :],
                       preferred_element_type=F32)
        if c == 0:
            acc_ref[...] = part
        else:
            acc_ref[...] += part
    return _post_residual(x, acc_ref[...], mod_ref, g_ref, sub, FFN_RES_WEIGHT)


def _ffn_inproj_kernel(x_ref, mod_ref, g_ref, *refs):
    (x1_ref, k1, v1, q4, k4, v4, q16, k16, v16, xr_ref, gr_ref,
     acc_ref, z_ref, z4_ref) = refs[3:17]
    win_ref, wout_ref, wmix_ref = _load_weights_once(refs[:3], refs[17:])
    x1 = _ffn(0, x_ref[...], mod_ref, g_ref, win_ref, wout_ref, acc_ref)
    x1_ref[...] = x1
    h = _pre(x1, mod_ref, g_ref, 1).astype(BF16)

    def proj(j, width):
        return jnp.dot(h, wmix_ref[:, j:j + width], preferred_element_type=F32)

    outs = ((None, q4, q16), (k1, k4, k16), (v1, v4, v16))
    for n, (o1, o4, o16) in enumerate(outs):
        z = proj(n * D_ATTN, D_ATTN)
        if n == 0:
            z = z * (HEAD_DIM ** -0.5 * math.log2(math.e))
        for t in range(D_ATTN // LANES):
            cols = slice(t * LANES, (t + 1) * LANES)
            if o1 is not None:
                o1[:, cols] = z[:, cols].astype(BF16)
            z_ref[n, t] = z[:, cols]
            for r4 in range(4):
                z4 = z_ref[n, t, pl.ds(r4, ROW_TILE // 4, stride=4), :]
                o4[0, r4, :, cols] = z4.astype(BF16)
                z4_ref[n, t, r4] = z4
                for e in range(4):
                    z16 = z4_ref[n, t, r4, pl.ds(e, ROW_TILE // 16, stride=4), :]
                    o16[0, r4 + 4 * e, :, cols] = z16.astype(BF16)

    for n, dst in enumerate((xr_ref, gr_ref)):
        z = proj(3 * D_ATTN + n * D_RNN, D_RNN)
        for t in range(D_RNN // LANES):
            for s in range(8):
                dst[0, t, pl.ds(s, SEG_LEN, stride=8), :] = z[
                    s * SEG_LEN:(s + 1) * SEG_LEN, t * LANES:(t + 1) * LANES]


def _ffn_inproj_call(x, mod9, g, w_in, w_out, w_mix):
    s = x.shape[0]
    row = pl.BlockSpec((ROW_TILE, D_MODEL), lambda i: (i, 0))
    nat = pl.BlockSpec((ROW_TILE, D_ATTN), lambda i: (i, 0))
    c4 = pl.BlockSpec((1, 4, ROW_TILE // 4, D_ATTN), lambda i: (i, 0, 0, 0))
    per16 = ATTN_CHUNK // ROW_TILE
    c16 = pl.BlockSpec((1, 16, ROW_TILE // 16, D_ATTN),
                       lambda i: (i // per16, 0, i % per16, 0))
    s_nat = jax.ShapeDtypeStruct((s, D_ATTN), BF16)
    s_c4 = jax.ShapeDtypeStruct((s // ROW_TILE, 4, ROW_TILE // 4, D_ATTN), BF16)
    s_c16 = jax.ShapeDtypeStruct((s // ATTN_CHUNK, 16, ATTN_BLOCK, D_ATTN), BF16)
    seg = pl.BlockSpec((1, D_RNN // LANES, ROW_TILE, LANES), lambda i: (i, 0, 0, 0))
    s_seg = jax.ShapeDtypeStruct((s // ROW_TILE, D_RNN // LANES, ROW_TILE, LANES), F32)
    weights = (w_in, w_out, w_mix)
    return pl.pallas_call(
        _ffn_inproj_kernel,
        out_shape=([jax.ShapeDtypeStruct(x.shape, F32)]
                   + [s_nat] * 2 + [s_c4] * 3 + [s_c16] * 3 + [s_seg] * 2),
        grid=(s // ROW_TILE,),
        in_specs=[row, _resident(mod9.shape), _resident(g.shape)]
                 + [pl.BlockSpec(memory_space=pl.ANY)] * len(weights),
        out_specs=[row] + [nat] * 2 + [c4] * 3 + [c16] * 3 + [seg] * 2,
        scratch_shapes=[pltpu.VMEM((ROW_TILE, D_MODEL), F32),
                        pltpu.VMEM((3, D_ATTN // LANES, ROW_TILE, LANES), F32),
                        pltpu.VMEM((3, D_ATTN // LANES, 4, ROW_TILE // 4, LANES), F32)]
                       + _weight_scratch(weights),
        compiler_params=pltpu.CompilerParams(
            dimension_semantics=("arbitrary",), vmem_limit_bytes=58 * MIB),
        name="ffn_inproj",
    )(x, mod9, g, w_in, w_out, w_mix)


def _attn_kernel(k1, v1, k1p, v1p, q4, k4, v4, k4p, v4p,
                 q16, k16, v16, k16p, v16p, o_ref,
                 kc1, vc1, kc4, vc4, kc16, vc16,
                 bias_ref, s_ref, p_ref, max_ref, acc_ref, den_ref, nat_ref):
    chunk = pl.program_id(0)
    nblk = ATTN_CHUNK // ATTN_BLOCK
    sub_rows = 4 * ATTN_BLOCK
    quarter = ATTN_BLOCK // 4
    lane = lax.broadcasted_iota(jnp.int32, (1, PAIR), 1)
    head_mask = [(lane < HEAD_DIM).astype(BF16), (lane >= HEAD_DIM).astype(BF16)]

    @pl.when((chunk == 0) & (pl.program_id(1) == 0))
    def _():
        for vc in (vc1, vc4, vc16):
            for h in range(2):
                vc[h, :, :, PAIR:] = jnp.broadcast_to(
                    head_mask[h], vc.shape[1:3] + (PAIR,))
        row = lax.broadcasted_iota(jnp.int32, (ATTN_BLOCK, 2 * ATTN_BLOCK), 0)
        col = lax.broadcasted_iota(jnp.int32, (ATTN_BLOCK, 2 * ATTN_BLOCK), 1)
        row4 = ((row & (quarter - 1)) << 2) | (row >> 5)
        for kind, r in enumerate((row, row4)):
            band = (col >= r) & (col <= r + N_BACK)
            bias_ref[2 * kind] = jnp.where(band, 0.0, -jnp.inf)
            bias_ref[2 * kind + 1] = jnp.where(band & (col >= ATTN_BLOCK), 0.0, -jnp.inf)

    def put(kc, vc, cls, rows, k, v):
        kc[cls, rows] = k
        for h in range(2):
            vc[h, cls, rows, :PAIR] = v * head_mask[h]

    prev = slice(0, ATTN_BLOCK)
    put(kc1, vc1, 0, prev, k1p[...], v1p[...])
    put(kc1, vc1, 0, slice(ATTN_BLOCK, None), k1[...], v1[...])
    for r in range(4):
        put(kc4, vc4, r, prev, k4p[0, r], v4p[0, r])
        for i in range(4):
            rows = slice((i + 1) * ATTN_BLOCK, (i + 2) * ATTN_BLOCK)
            put(kc4, vc4, r, rows, k4[i, r], v4[i, r])
    for r in range(16):
        put(kc16, vc16, r, prev, k16p[0, r], v16p[0, r])
        put(kc16, vc16, r, slice(ATTN_BLOCK, None), k16[0, r], v16[0, r])

    max_ref[...] = jnp.full(max_ref.shape, -jnp.inf, F32)
    acc_ref[...] = jnp.zeros(acc_ref.shape, F32)
    den_ref[...] = jnp.zeros(den_ref.shape, F32)

    def window(j):
        return pl.ds(j * ATTN_BLOCK, 2 * ATTN_BLOCK)

    def dilation1(j):
        i, jj = divmod(j, 4)
        rows = slice(jj * quarter, (jj + 1) * quarter)
        q = jnp.concatenate([q4[i, r4, rows, :] for r4 in range(4)], axis=0)
        pieces = [(pl.ds(i * sub_rows + r4 * ATTN_BLOCK + jj * quarter, quarter),
                   slice(r4 * quarter, (r4 + 1) * quarter)) for r4 in range(4)]
        return (q, lambda: kc1[0, window(j), :], lambda h: vc1[h, 0, window(j), :],
                True if j > 0 else chunk > 0, 1, pieces)

    def dilation4(t):
        i, r4 = divmod(t, 4)
        pieces = [(pl.ds(i * sub_rows + r4 * ATTN_BLOCK, ATTN_BLOCK),
                   slice(0, ATTN_BLOCK))]
        return (q4[i, r4], lambda: kc4[r4, window(i), :],
                lambda h: vc4[h, r4, window(i), :],
                True if i > 0 else chunk > 0, 0, pieces)

    def dilation16(r):
        e, r4 = divmod(r, 4)
        pieces = [(pl.ds(i * sub_rows + r4 * ATTN_BLOCK + e, quarter, stride=4),
                   slice(i * quarter, (i + 1) * quarter)) for i in range(4)]
        return (q16[0, r], lambda: kc16[r], lambda h: vc16[h, r],
                chunk > 0, 0, pieces)

    makers = [dilation4] * nblk + [dilation16] * nblk + [dilation1] * nblk
    n_tiles = len(makers)

    def scores(t):
        q, keys, _, has_prev, kind, pieces = makers[t](t % nblk)
        no_prev = 0 if has_prev is True else jnp.where(has_prev, 0, 1)
        bias = bias_ref[2 * kind + no_prev]
        kk = keys()
        for h in range(2):
            s = lax.dot_general(q * head_mask[h], kk, (((1,), (1,)), ((), ())),
                                preferred_element_type=F32) + bias
            s_ref[t, h] = s
            m = jnp.broadcast_to(jnp.max(s, axis=-1, keepdims=True),
                                 (ATTN_BLOCK, PAIR))
            for canon, rows in pieces:
                max_ref[h, canon, :] = jnp.maximum(max_ref[h, canon, :], m[rows])

    def softmax(t):
        pieces = makers[t](t % nblk)[5]
        for h in range(2):
            m = jnp.concatenate([max_ref[h, canon, :] for canon, _ in pieces], axis=0)
            p = jnp.exp2(s_ref[t, h] - jnp.concatenate([m, m], axis=1))
            p_ref[t % nblk, h] = p.astype(BF16)

    def values(t):
        _, _, vals, _, _, pieces = makers[t](t % nblk)
        pv = (jnp.dot(p_ref[t % nblk, 0], vals(0), preferred_element_type=F32)
              + jnp.dot(p_ref[t % nblk, 1], vals(1), preferred_element_type=F32))
        for canon, rows in pieces:
            acc_ref[canon, :] += pv[rows, :PAIR]
            den_ref[canon, :] += pv[rows, PAIR:]

    def pipeline(stages):
        n_groups = n_tiles // ATTN_GROUP
        for i in range(n_groups + len(stages) - 1):
            for lag in reversed(range(len(stages))):
                if 0 <= i - lag < n_groups:
                    for j in range(ATTN_GROUP):
                        stages[lag]((i - lag) * ATTN_GROUP + j)

    pipeline([scores])
    pipeline([softmax, values])

    for i in range(ATTN_CHUNK // sub_rows):
        for r4 in range(4):
            canon = pl.ds(i * sub_rows + r4 * ATTN_BLOCK, ATTN_BLOCK)
            nat_ref[pl.ds(i * sub_rows + r4, ATTN_BLOCK, stride=4), :] = (
                acc_ref[canon, :] / den_ref[canon, :])
    o_ref[...] = nat_ref[...].astype(BF16)


def _attn_call(k1, v1, q4, k4, v4, q16, k16, v16):
    s = k1.shape[0]
    n_pairs = D_ATTN // PAIR
    blocks_per_chunk = ATTN_CHUNK // ATTN_BLOCK
    sub_per_chunk = ATTN_CHUNK // (4 * ATTN_BLOCK)

    nat = pl.BlockSpec((ATTN_CHUNK, PAIR), lambda c, p: (c, p))
    nat_prev = pl.BlockSpec(
        (ATTN_BLOCK, PAIR),
        lambda c, p: (jnp.maximum(c * blocks_per_chunk - 1, 0), p))
    c4 = pl.BlockSpec((sub_per_chunk, 4, ATTN_BLOCK, PAIR), lambda c, p: (c, 0, 0, p))
    c4_prev = pl.BlockSpec(
        (1, 4, ATTN_BLOCK, PAIR),
        lambda c, p: (jnp.maximum(c * sub_per_chunk - 1, 0), 0, 0, p))
    c16 = pl.BlockSpec((1, 16, ATTN_BLOCK, PAIR), lambda c, p: (c, 0, 0, p))
    c16_prev = pl.BlockSpec((1, 16, ATTN_BLOCK, PAIR),
                            lambda c, p: (jnp.maximum(c - 1, 0), 0, 0, p))
    kcat = lambda classes, blocks: pltpu.VMEM(
        (classes, (blocks + 1) * ATTN_BLOCK, PAIR), BF16)
    vcat = lambda classes, blocks: pltpu.VMEM(
        (2, classes, (blocks + 1) * ATTN_BLOCK, 2 * PAIR), BF16)
    rows = lambda: pltpu.VMEM((ATTN_CHUNK, PAIR), F32)
    return pl.pallas_call(
        _attn_kernel,
        out_shape=jax.ShapeDtypeStruct((s, D_ATTN), BF16),
        grid=(s // ATTN_CHUNK, n_pairs),
        in_specs=[nat] * 2 + [nat_prev] * 2 + [c4] * 3 + [c4_prev] * 2
                 + [c16] * 3 + [c16_prev] * 2,
        out_specs=nat,
        scratch_shapes=[
            kcat(1, 16), vcat(1, 16), kcat(4, 4), vcat(4, 4), kcat(16, 1), vcat(16, 1),
            pltpu.VMEM((4, ATTN_BLOCK, 2 * ATTN_BLOCK), F32),
            pltpu.VMEM((3 * blocks_per_chunk, 2, ATTN_BLOCK, 2 * ATTN_BLOCK), F32),
            pltpu.VMEM((blocks_per_chunk, 2, ATTN_BLOCK, 2 * ATTN_BLOCK), BF16),
            pltpu.VMEM((2, ATTN_CHUNK, PAIR), F32),
            rows(), rows(), rows()],
        compiler_params=pltpu.CompilerParams(
            dimension_semantics=("arbitrary", "arbitrary"),
            vmem_limit_bytes=56 * MIB),
        name="attn",
    )(k1, v1, k1, v1, q4, k4, v4, k4, v4, q16, k16, v16, k16, v16)


def _rglru_kernel(xr_ref, xp_ref, gr_ref, cw_ref, cb_ref, wx_ref, wa_ref,
                  bx_ref, ba_ref, lam_ref, o_ref,
                  xcat_ref, a_ref, b_ref, h_ref, nat_ref, carry_ref):
    step = pl.program_id(0)
    half = D_RNN // 2
    n_slabs = D_RNN // LANES
    halo = 8 * (CONV_WIDTH - 1)

    @pl.when(step == 0)
    def _():
        carry_ref[...] = jnp.zeros_like(carry_ref)

    def wide(ref, rows):
        return jnp.concatenate([ref[0, t, rows, :] for t in range(n_slabs)], axis=1)

    sub = lax.broadcasted_iota(jnp.int32, (8, D_RNN), 0)
    for d in range(1, CONV_WIDTH):
        cur = wide(xr_ref, pl.ds(8 * (SEG_LEN - d), 8))
        prv = jnp.where(step > 0, wide(xp_ref, pl.ds(8 * (CONV_WIDTH - d), 8)), 0.0)
        xcat_ref[pl.ds(halo - 8 * d, 8), :] = jnp.where(
            sub == 0, pltpu.roll(prv, 1, 0), pltpu.roll(cur, 1, 0))
    xcat_ref[halo:, :] = wide(xr_ref, slice(None))
    u = cb_ref[...]
    for k in range(CONV_WIDTH):
        d = CONV_WIDTH - 1 - k
        u = u + cw_ref[k:k + 1, :] * xcat_ref[pl.ds(halo - 8 * d, ROW_TILE), :]

    ub = u.astype(BF16)

    def gate(w_ref, b_ref):
        lo = jnp.dot(ub[:, :half], w_ref[0], preferred_element_type=F32)
        hi = jnp.dot(ub[:, half:], w_ref[1], preferred_element_type=F32)
        z = jnp.concatenate([lo, hi], axis=1) + b_ref[...]
        return 0.5 * jnp.tanh(0.5 * z) + 0.5

    gate_x = gate(wx_ref, bx_ref)
    gate_a = gate(wa_ref, ba_ref)
    neg_lam = -lam_ref[...]
    softplus = jnp.maximum(neg_lam, 0.0) + jnp.log1p(jnp.exp(-jnp.abs(neg_lam)))
    log_a = -LRU_C * gate_a * softplus
    a = jnp.exp(log_a)
    one_minus_a2 = -jnp.tanh(log_a) * (a * a + 1.0)
    root = jnp.where(one_minus_a2 > 0.0, one_minus_a2 * lax.rsqrt(one_minus_a2), 0.0)
    a_ref[...] = a
    b_ref[...] = u * gate_x * root

    def advance(j, carry):
        h_loc, prod = carry
        rows = pl.ds(pl.multiple_of(j * 8, 8), 8)
        a_j = a_ref[rows, :]
        h_loc = a_j * h_loc + b_ref[rows, :]
        prod = a_j * prod
        h_ref[rows, :] = h_loc
        a_ref[rows, :] = prod
        return h_loc, prod

    seg_end, seg_prod = lax.fori_loop(
        0, SEG_LEN, advance,
        (jnp.zeros((8, D_RNN), F32), jnp.ones((8, D_RNN), F32)), unroll=8)

    entering = [carry_ref[0:1, :]]
    for s in range(8):
        entering.append(seg_end[s:s + 1, :] + seg_prod[s:s + 1, :] * entering[s])
    carry_ref[0:1, :] = entering[8]
    state_in = jnp.tile(jnp.concatenate(entering[:8], axis=0), (SEG_LEN, 1))

    g = wide(gr_ref, slice(None))
    g_half = 0.5 * g
    inner = g * (math.sqrt(2.0 / math.pi)
                 + (0.044715 * math.sqrt(2.0 / math.pi)) * (g * g))
    gelu = g_half + g_half * jnp.tanh(inner)
    out = (h_ref[...] + a_ref[...] * state_in) * gelu

    for t in range(n_slabs):
        nat_ref[t] = out[:, t * LANES:(t + 1) * LANES]
        for s in range(8):
            o_ref[s * SEG_LEN:(s + 1) * SEG_LEN, t * LANES:(t + 1) * LANES] = (
                nat_ref[t, pl.ds(s, SEG_LEN, stride=8), :].astype(BF16))


def _rglru_call(xr, gr, conv_w, conv_b, wx, wa, b_x, b_a, lam):
    n_tiles, n_slabs = xr.shape[0], xr.shape[1]
    tile = pl.BlockSpec((1, n_slabs, ROW_TILE, LANES), lambda i: (i, 0, 0, 0))
    last_rows = 8 * CONV_WIDTH
    prev = pl.BlockSpec(
        (1, n_slabs, last_rows, LANES),
        lambda i: (jnp.maximum(i - 1, 0), 0, ROW_TILE // last_rows - 1, 0))
    vec = _resident((1, D_RNN))
    return pl.pallas_call(
        _rglru_kernel,
        out_shape=jax.ShapeDtypeStruct((n_tiles * ROW_TILE, D_RNN), BF16),
        grid=(n_tiles,),
        in_specs=[tile, prev, tile, _resident(conv_w.shape), vec,
                  _resident(wx.shape), _resident(wa.shape), vec, vec, vec],
        out_specs=pl.BlockSpec((ROW_TILE, D_RNN), lambda i: (i, 0)),
        scratch_shapes=[pltpu.VMEM((ROW_TILE + 8 * (CONV_WIDTH - 1), D_RNN), F32),
                        pltpu.VMEM((ROW_TILE, D_RNN), F32),
                        pltpu.VMEM((ROW_TILE, D_RNN), F32),
                        pltpu.VMEM((ROW_TILE, D_RNN), F32),
                        pltpu.VMEM((n_slabs, ROW_TILE, LANES), F32),
                        pltpu.VMEM((8, D_RNN), F32)],
        compiler_params=pltpu.CompilerParams(
            dimension_semantics=("arbitrary",), vmem_limit_bytes=32 * MIB),
        name="rglru",
    )(xr, xr, gr, conv_w, conv_b, wx, wa, b_x, b_a, lam)


def _outproj_ffn_kernel(x_ref, attn_ref, rec_ref, mod_ref, g_ref, *refs):
    o_ref, acc_ref = refs[3:5]
    wmix_ref, win_ref, wout_ref = _load_weights_once(refs[:3], refs[5:])
    y = jnp.dot(attn_ref[...], wmix_ref[:D_ATTN, :], preferred_element_type=F32)
    y = y + jnp.dot(rec_ref[...], wmix_ref[D_ATTN:, :], preferred_element_type=F32)
    x2 = _post_residual(x_ref[...], y, mod_ref, g_ref, 1, 1.0)
    o_ref[...] = _ffn(2, x2, mod_ref, g_ref, win_ref, wout_ref, acc_ref)


def _outproj_ffn_call(x, attn, rec, mod9, g, w_mix, w_in, w_out):
    row = pl.BlockSpec((ROW_TILE, D_MODEL), lambda i: (i, 0))
    half = pl.BlockSpec((ROW_TILE, D_ATTN), lambda i: (i, 0))
    weights = (w_mix, w_in, w_out)
    return pl.pallas_call(
        _outproj_ffn_kernel,
        out_shape=jax.ShapeDtypeStruct(x.shape, F32),
        grid=(x.shape[0] // ROW_TILE,),
        in_specs=[row, half, half, _resident(mod9.shape), _resident(g.shape)]
                 + [pl.BlockSpec(memory_space=pl.ANY)] * len(weights),
        out_specs=row,
        scratch_shapes=[pltpu.VMEM((ROW_TILE, D_MODEL), F32)]
                       + _weight_scratch(weights),
        compiler_params=pltpu.CompilerParams(
            dimension_semantics=("arbitrary",), vmem_limit_bytes=52 * MIB),
        name="outproj_ffn",
    )(x, attn, rec, mod9, g, w_mix, w_in, w_out)


def _blockdiag_halves(w):
    per_half = (D_RNN // 2) // LRU_BLOCK
    eye = jnp.eye(per_half, dtype=w.dtype)
    wh = w.reshape(2, per_half, LRU_BLOCK, LRU_BLOCK)
    dense = jnp.einsum('hgij,gk->hgikj', wh, eye)
    return dense.reshape(2, D_RNN // 2, D_RNN // 2).astype(BF16)


def kernel(x, c, w_ada, b_ada, norm_gain, ffn1_w_in, ffn1_w_out, mix_w_in, conv_w, conv_b,
           lru_w_x, lru_b_x, lru_w_a, lru_b_a, lru_lambda, mix_w_out, ffn2_w_in, ffn2_w_out):
    batch, seq, d = x.shape
    assert (batch, seq, d) == (1, SEQ, D_MODEL) and w_ada.shape[0] == 1
    xs = x.reshape(seq, d)
    g = norm_gain[0]

    mod = _mod_call(c.reshape(d, 1), w_ada[0], b_ada)
    mod9 = mod.reshape(9, d)

    x1, *qkv, xr, gr = _ffn_inproj_call(
        xs, mod9, g, ffn1_w_in[0], ffn1_w_out[0], mix_w_in[0])
    attn = _attn_call(*qkv)
    rec = _rglru_call(xr, gr, conv_w[0], conv_b, _blockdiag_halves(lru_w_x[0]),
                      _blockdiag_halves(lru_w_a[0]), lru_b_x, lru_b_a, lru_lambda)
    x3 = _outproj_ffn_call(x1, attn, rec, mod9, g, mix_w_out[0],
                           ffn2_w_in[0], ffn2_w_out[0])
    return x3.reshape(batch, seq, d)
```

```python
import math

import jax
import jax.numpy as jnp
from jax import lax
from jax.experimental import pallas as pl
from jax.experimental.pallas import tpu as pltpu

F32 = jnp.float32
BF16 = jnp.bfloat16

D_MODEL = 1024
SEQ = 16384
HEAD_DIM = 64
D_ATTN = 512
D_RNN = 512
LRU_BLOCK = 64
N_BACK = 128
ATTN_BLOCK = 128
CONV_WIDTH = 4
LRU_C = 8.0
D_FF = 2816
FFN_RES_WEIGHT = 0.5
RMS_EPS = 1e-6

MIB = 1024 * 1024
ROW_TILE = 512
OUT_ROW_TILE = 1024
SEG_LEN = ROW_TILE // 8
FF_CHUNK = 256
ATTN_CHUNK = 2048
LANES = 128
STAGE_ROWS, STAGE_COLS = 256, 512
STAGE_SLOTS = 8
PAIR = 2 * HEAD_DIM
MOD_TILE = 1152
ATTN_GROUP = 4


def _resident(shape):
    return pl.BlockSpec(shape, lambda *_: (0,) * len(shape),
                        pipeline_mode=pl.Buffered(1))


def _weight_scratch(weights):
    for w in weights:
        assert w.shape[0] % STAGE_ROWS == 0 and w.shape[1] % STAGE_COLS == 0, w.shape
    return ([pltpu.VMEM(w.shape, BF16) for w in weights]
            + [pltpu.VMEM((STAGE_SLOTS, STAGE_ROWS, STAGE_COLS), F32),
               pltpu.SemaphoreType.DMA((STAGE_SLOTS,))])


def _load_weights_once(hbm_refs, scratch_refs):
    *dst_refs, stage_ref, sem = scratch_refs
    blocks = [(hbm, dst, r, c)
              for hbm, dst in zip(hbm_refs, dst_refs)
              for r in range(0, dst.shape[0], STAGE_ROWS)
              for c in range(0, dst.shape[1], STAGE_COLS)]

    def copy(i):
        hbm, _, r, c = blocks[i]
        slot = i % STAGE_SLOTS
        return pltpu.make_async_copy(
            hbm.at[pl.ds(r, STAGE_ROWS), pl.ds(c, STAGE_COLS)],
            stage_ref.at[slot], sem.at[slot])

    @pl.when(pl.program_id(0) == 0)
    def _():
        for i in range(min(STAGE_SLOTS, len(blocks))):
            copy(i).start()
        for i, (_, dst, r, c) in enumerate(blocks):
            copy(i).wait()
            dst[r:r + STAGE_ROWS, c:c + STAGE_COLS] = (
                stage_ref[i % STAGE_SLOTS].astype(BF16))
            if i + STAGE_SLOTS < len(blocks):
                copy(i + STAGE_SLOTS).start()

    return dst_refs


def _rms(v):
    return v * lax.rsqrt(jnp.mean(v * v, axis=-1, keepdims=True) + RMS_EPS)


def _pre(x, mod_ref, g_ref, sub):
    shift = mod_ref[3 * sub:3 * sub + 1, :]
    scale = mod_ref[3 * sub + 1:3 * sub + 2, :]
    gain = g_ref[2 * sub:2 * sub + 1, :] * (1.0 + scale)
    return _rms(x) * gain + shift


def _post_residual(x, y, mod_ref, g_ref, sub, weight):
    gate = mod_ref[3 * sub + 2:3 * sub + 3, :]
    gain = weight * (1.0 + gate) * g_ref[2 * sub + 1:2 * sub + 2, :]
    return x + _rms(y) * gain


def _mod_kernel(c_ref, w_ref, b_ref, o_ref):
    c = c_ref[...]
    s = c * jax.nn.sigmoid(c)
    o_ref[...] = jnp.sum(s * w_ref[...], axis=0, keepdims=True) + b_ref[...]


def _mod_call(c_col, w_ada, b_ada):
    n = w_ada.shape[1]
    return pl.pallas_call(
        _mod_kernel,
        out_shape=jax.ShapeDtypeStruct((1, n), F32),
        grid=(n // MOD_TILE,),
        in_specs=[
            pl.BlockSpec((D_MODEL, 1), lambda j: (0, 0)),
            pl.BlockSpec((D_MODEL, MOD_TILE), lambda j: (0, j)),
            pl.BlockSpec((1, MOD_TILE), lambda j: (0, j)),
        ],
        out_specs=pl.BlockSpec((1, MOD_TILE), lambda j: (0, j)),
        compiler_params=pltpu.CompilerParams(
            dimension_semantics=("arbitrary",), vmem_limit_bytes=24 * MIB),
        name="mod",
    )(c_col, w_ada, b_ada)


def _ffn(sub, x, mod_ref, g_ref, win_ref, wout_ref, acc_ref):
    h = _pre(x, mod_ref, g_ref, sub).astype(BF16)
    for lo in range(0, D_FF, FF_CHUNK):
        hi = min(lo + FF_CHUNK, D_FF)
        hg = jnp.dot(h, win_ref[:, lo:hi], preferred_element_type=F32)
        hu = jnp.dot(h, win_ref[:, D_FF + lo:D_FF + hi], preferred_element_type=F32)
        a = (hg * jax.nn.sigmoid(hg) * hu).astype(BF16)
        part = jnp.dot(a, wout_ref[lo:hi, :], preferred_element_type=F32)
        if lo == 0:
            acc_ref[...] = part
        else:
            acc_ref[...] += part
    return _post_residual(x, acc_ref[...], mod_ref, g_ref, sub, FFN_RES_WEIGHT)


def _ffn_inproj_kernel(x_ref, mod_ref, g_ref, *refs):
    (x1_ref, k1, v1, q4, k4, v4, q16, k16, v16, xr_ref, gr_ref,
     acc_ref, z_ref, z4_ref) = refs[3:17]
    win_ref, wout_ref, wmix_ref = _load_weights_once(refs[:3], refs[17:])
    x1 = _ffn(0, x_ref[...], mod_ref, g_ref, win_ref, wout_ref, acc_ref)
    x1_ref[...] = x1
    h = _pre(x1, mod_ref, g_ref, 1).astype(BF16)

    def proj(j, width):
        return jnp.dot(h, wmix_ref[:, j:j + width], preferred_element_type=F32)

    outs = ((None, q4, q16), (k1, k4, k16), (v1, v4, v16))
    for n, (o1, o4, o16) in enumerate(outs):
        z = proj(n * D_ATTN, D_ATTN)
        if n == 0:
            z = z * (HEAD_DIM ** -0.5 * math.log2(math.e))
        for t in range(D_ATTN // LANES):
            cols = slice(t * LANES, (t + 1) * LANES)
            if o1 is not None:
                o1[:, cols] = z[:, cols].astype(BF16)
            z_ref[n, t] = z[:, cols]
            for r4 in range(4):
                z4 = z_ref[n, t, pl.ds(r4, ROW_TILE // 4, stride=4), :]
                o4[0, r4, :, cols] = z4.astype(BF16)
                z4_ref[n, t, r4] = z4
                for e in range(4):
                    z16 = z4_ref[n, t, r4, pl.ds(e, ROW_TILE // 16, stride=4), :]
                    o16[0, r4 + 4 * e, :, cols] = z16.astype(BF16)

    for n, dst in enumerate((xr_ref, gr_ref)):
        z = proj(3 * D_ATTN + n * D_RNN, D_RNN)
        for t in range(D_RNN // LANES):
            for s in range(8):
                dst[0, t, pl.ds(s, SEG_LEN, stride=8), :] = z[
                    s * SEG_LEN:(s + 1) * SEG_LEN, t * LANES:(t + 1) * LANES]


def _ffn_inproj_call(x, mod9, g, w_in, w_out, w_mix):
    s = x.shape[0]
    row = pl.BlockSpec((ROW_TILE, D_MODEL), lambda i: (i, 0))
    nat = pl.BlockSpec((ROW_TILE, D_ATTN), lambda i: (i, 0))
    c4 = pl.BlockSpec((1, 4, ROW_TILE // 4, D_ATTN), lambda i: (i, 0, 0, 0))
    per16 = ATTN_CHUNK // ROW_TILE
    c16 = pl.BlockSpec((1, 16, ROW_TILE // 16, D_ATTN),
                       lambda i: (i // per16, 0, i % per16, 0))
    s_nat = jax.ShapeDtypeStruct((s, D_ATTN), BF16)
    s_c4 = jax.ShapeDtypeStruct((s // ROW_TILE, 4, ROW_TILE // 4, D_ATTN), BF16)
    s_c16 = jax.ShapeDtypeStruct((s // ATTN_CHUNK, 16, ATTN_BLOCK, D_ATTN), BF16)
    seg = pl.BlockSpec((1, D_RNN // LANES, ROW_TILE, LANES), lambda i: (i, 0, 0, 0))
    s_seg = jax.ShapeDtypeStruct((s // ROW_TILE, D_RNN // LANES, ROW_TILE, LANES), F32)
    weights = (w_in, w_out, w_mix)
    return pl.pallas_call(
        _ffn_inproj_kernel,
        out_shape=([jax.ShapeDtypeStruct(x.shape, F32)]
                   + [s_nat] * 2 + [s_c4] * 3 + [s_c16] * 3 + [s_seg] * 2),
        grid=(s // ROW_TILE,),
        in_specs=[row, _resident(mod9.shape), _resident(g.shape)]
                 + [pl.BlockSpec(memory_space=pl.ANY)] * len(weights),
        out_specs=[row] + [nat] * 2 + [c4] * 3 + [c16] * 3 + [seg] * 2,
        scratch_shapes=[pltpu.VMEM((ROW_TILE, D_MODEL), F32),
                        pltpu.VMEM((3, D_ATTN // LANES, ROW_TILE, LANES), F32),
                        pltpu.VMEM((3, D_ATTN // LANES, 4, ROW_TILE // 4, LANES), F32)]
                       + _weight_scratch(weights),
        compiler_params=pltpu.CompilerParams(
            dimension_semantics=("arbitrary",), vmem_limit_bytes=58 * MIB),
        name="ffn_inproj",
    )(x, mod9, g, w_in, w_out, w_mix)


def _attn_kernel(k1, v1, k1p, v1p, q4, k4, v4, k4p, v4p,
                 q16, k16, v16, k16p, v16p, o_ref,
                 kc1, vc1, kc4, vc4, kc16, vc16,
                 bias_ref, s_ref, p_ref, max_ref, acc_ref, den_ref, nat_ref):
    chunk = pl.program_id(0)
    nblk = ATTN_CHUNK // ATTN_BLOCK
    sub_rows = 4 * ATTN_BLOCK
    quarter = ATTN_BLOCK // 4
    lane = lax.broadcasted_iota(jnp.int32, (1, PAIR), 1)
    head_mask = [(lane < HEAD_DIM).astype(BF16), (lane >= HEAD_DIM).astype(BF16)]

    @pl.when((chunk == 0) & (pl.program_id(1) == 0))
    def _():
        for vc in (vc1, vc4, vc16):
            for h in range(2):
                vc[h, :, :, PAIR:] = jnp.broadcast_to(
                    head_mask[h], vc.shape[1:3] + (PAIR,))
        row = lax.broadcasted_iota(jnp.int32, (ATTN_BLOCK, 2 * ATTN_BLOCK), 0)
        col = lax.broadcasted_iota(jnp.int32, (ATTN_BLOCK, 2 * ATTN_BLOCK), 1)
        row4 = ((row & (quarter - 1)) << 2) | (row >> 5)
        for kind, r in enumerate((row, row4)):
            band = (col >= r) & (col <= r + N_BACK)
            bias_ref[2 * kind] = jnp.where(band, 0.0, -jnp.inf)
            bias_ref[2 * kind + 1] = jnp.where(band & (col >= ATTN_BLOCK), 0.0, -jnp.inf)

    def put(kc, vc, cls, rows, k, v):
        kc[cls, rows] = k
        for h in range(2):
            vc[h, cls, rows, :PAIR] = v * head_mask[h]

    prev = slice(0, ATTN_BLOCK)
    put(kc1, vc1, 0, prev, k1p[...], v1p[...])
    put(kc1, vc1, 0, slice(ATTN_BLOCK, None), k1[...], v1[...])
    for r in range(4):
        put(kc4, vc4, r, prev, k4p[0, r], v4p[0, r])
        for i in range(4):
            rows = slice((i + 1) * ATTN_BLOCK, (i + 2) * ATTN_BLOCK)
            put(kc4, vc4, r, rows, k4[i, r], v4[i, r])
    for r in range(16):
        put(kc16, vc16, r, prev, k16p[0, r], v16p[0, r])
        put(kc16, vc16, r, slice(ATTN_BLOCK, None), k16[0, r], v16[0, r])

    max_ref[...] = jnp.full(max_ref.shape, -jnp.inf, F32)
    acc_ref[...] = jnp.zeros(acc_ref.shape, F32)
    den_ref[...] = jnp.zeros(den_ref.shape, F32)

    def window(j):
        return pl.ds(j * ATTN_BLOCK, 2 * ATTN_BLOCK)

    def dilation1(j):
        i, jj = divmod(j, 4)
        rows = slice(jj * quarter, (jj + 1) * quarter)
        q = jnp.concatenate([q4[i, r4, rows, :] for r4 in range(4)], axis=0)
        pieces = [(pl.ds(i * sub_rows + r4 * ATTN_BLOCK + jj * quarter, quarter),
                   slice(r4 * quarter, (r4 + 1) * quarter)) for r4 in range(4)]
        return (q, lambda: kc1[0, window(j), :], lambda h: vc1[h, 0, window(j), :],
                True if j > 0 else chunk > 0, 1, pieces)

    def dilation4(t):
        i, r4 = divmod(t, 4)
        pieces = [(pl.ds(i * sub_rows + r4 * ATTN_BLOCK, ATTN_BLOCK),
                   slice(0, ATTN_BLOCK))]
        return (q4[i, r4], lambda: kc4[r4, window(i), :],
                lambda h: vc4[h, r4, window(i), :],
                True if i > 0 else chunk > 0, 0, pieces)

    def dilation16(r):
        e, r4 = divmod(r, 4)
        pieces = [(pl.ds(i * sub_rows + r4 * ATTN_BLOCK + e, quarter, stride=4),
                   slice(i * quarter, (i + 1) * quarter)) for i in range(4)]
        return (q16[0, r], lambda: kc16[r], lambda h: vc16[h, r],
                chunk > 0, 0, pieces)

    makers = [dilation4] * nblk + [dilation16] * nblk + [dilation1] * nblk
    n_tiles = len(makers)

    def scores(t):
        q, keys, _, has_prev, kind, pieces = makers[t](t % nblk)
        no_prev = 0 if has_prev is True else jnp.where(has_prev, 0, 1)
        bias = bias_ref[2 * kind + no_prev]
        kk = keys()
        for h in range(2):
            s = lax.dot_general(q * head_mask[h], kk, (((1,), (1,)), ((), ())),
                                preferred_element_type=F32) + bias
            s_ref[t, h] = s
            m = jnp.broadcast_to(jnp.max(s, axis=-1, keepdims=True),
                                 (ATTN_BLOCK, PAIR))
            for canon, rows in pieces:
                max_ref[h, canon, :] = jnp.maximum(max_ref[h, canon, :], m[rows])

    def softmax(t):
        pieces = makers[t](t % nblk)[5]
        for h in range(2):
            m = jnp.concatenate([max_ref[h, canon, :] for canon, _ in pieces], axis=0)
            p = jnp.exp2(s_ref[t, h] - jnp.concatenate([m, m], axis=1))
            p_ref[t % nblk, h] = p.astype(BF16)

    def values(t):
        _, _, vals, _, _, pieces = makers[t](t % nblk)
        pv = (jnp.dot(p_ref[t % nblk, 0], vals(0), preferred_element_type=F32)
              + jnp.dot(p_ref[t % nblk, 1], vals(1), preferred_element_type=F32))
        for canon, rows in pieces:
            acc_ref[canon, :] += pv[rows, :PAIR]
            den_ref[canon, :] += pv[rows, PAIR:]

    def pipeline(stages):
        n_groups = n_tiles // ATTN_GROUP
        for i in range(n_groups + len(stages) - 1):
            for lag in reversed(range(len(stages))):
                if 0 <= i - lag < n_groups:
                    for j in range(ATTN_GROUP):
                        stages[lag]((i - lag) * ATTN_GROUP + j)

    pipeline([scores])
    pipeline([softmax, values])

    for i in range(ATTN_CHUNK // sub_rows):
        for r4 in range(4):
            canon = pl.ds(i * sub_rows + r4 * ATTN_BLOCK, ATTN_BLOCK)
            nat_ref[pl.ds(i * sub_rows + r4, ATTN_BLOCK, stride=4), :] = (
                acc_ref[canon, :] / den_ref[canon, :])
    o_ref[...] = nat_ref[...].astype(BF16)


def _attn_call(k1, v1, q4, k4, v4, q16, k16, v16):
    s = k1.shape[0]
    n_pairs = D_ATTN // PAIR
    blocks_per_chunk = ATTN_CHUNK // ATTN_BLOCK
    sub_per_chunk = ATTN_CHUNK // (4 * ATTN_BLOCK)

    nat = pl.BlockSpec((ATTN_CHUNK, PAIR), lambda c, p: (c, p))
    nat_prev = pl.BlockSpec(
        (ATTN_BLOCK, PAIR),
        lambda c, p: (jnp.maximum(c * blocks_per_chunk - 1, 0), p))
    c4 = pl.BlockSpec((sub_per_chunk, 4, ATTN_BLOCK, PAIR), lambda c, p: (c, 0, 0, p))
    c4_prev = pl.BlockSpec(
        (1, 4, ATTN_BLOCK, PAIR),
        lambda c, p: (jnp.maximum(c * sub_per_chunk - 1, 0), 0, 0, p))
    c16 = pl.BlockSpec((1, 16, ATTN_BLOCK, PAIR), lambda c, p: (c, 0, 0, p))
    c16_prev = pl.BlockSpec((1, 16, ATTN_BLOCK, PAIR),
                            lambda c, p: (jnp.maximum(c - 1, 0), 0, 0, p))
    kcat = lambda classes, blocks: pltpu.VMEM(
        (classes, (blocks + 1) * ATTN_BLOCK, PAIR), BF16)
    vcat = lambda classes, blocks: pltpu.VMEM(
        (2, classes, (blocks + 1) * ATTN_BLOCK, 2 * PAIR), BF16)
    rows = lambda: pltpu.VMEM((ATTN_CHUNK, PAIR), F32)
    return pl.pallas_call(
        _attn_kernel,
        out_shape=jax.ShapeDtypeStruct((s, D_ATTN), BF16),
        grid=(s // ATTN_CHUNK, n_pairs),
        in_specs=[nat] * 2 + [nat_prev] * 2 + [c4] * 3 + [c4_prev] * 2
                 + [c16] * 3 + [c16_prev] * 2,
        out_specs=nat,
        scratch_shapes=[
            kcat(1, 16), vcat(1, 16), kcat(4, 4), vcat(4, 4), kcat(16, 1), vcat(16, 1),
            pltpu.VMEM((4, ATTN_BLOCK, 2 * ATTN_BLOCK), F32),
            pltpu.VMEM((3 * blocks_per_chunk, 2, ATTN_BLOCK, 2 * ATTN_BLOCK), F32),
            pltpu.VMEM((blocks_per_chunk, 2, ATTN_BLOCK, 2 * ATTN_BLOCK), BF16),
            pltpu.VMEM((2, ATTN_CHUNK, PAIR), F32),
            rows(), rows(), rows()],
        compiler_params=pltpu.CompilerParams(
            dimension_semantics=("arbitrary", "arbitrary"),
            vmem_limit_bytes=56 * MIB),
        name="attn",
    )(k1, v1, k1, v1, q4, k4, v4, k4, v4, q16, k16, v16, k16, v16)


def _rglru_kernel(xr_ref, xp_ref, gr_ref, cw_ref, cb_ref, wx_ref, wa_ref,
                  bx_ref, ba_ref, lam_ref, o_ref,
                  xcat_ref, a_ref, b_ref, h_ref, nat_ref, carry_ref):
    step = pl.program_id(0)
    half = D_RNN // 2
    n_slabs = D_RNN // LANES
    halo = 8 * (CONV_WIDTH - 1)

    @pl.when(step == 0)
    def _():
        carry_ref[...] = jnp.zeros_like(carry_ref)

    def wide(ref, rows):
        return jnp.concatenate([ref[0, t, rows, :] for t in range(n_slabs)], axis=1)

    sub = lax.broadcasted_iota(jnp.int32, (8, D_RNN), 0)
    for d in range(1, CONV_WIDTH):
        cur = wide(xr_ref, pl.ds(8 * (SEG_LEN - d), 8))
        prv = jnp.where(step > 0, wide(xp_ref, pl.ds(8 * (CONV_WIDTH - d), 8)), 0.0)
        xcat_ref[pl.ds(halo - 8 * d, 8), :] = jnp.where(
            sub == 0, pltpu.roll(prv, 1, 0), pltpu.roll(cur, 1, 0))
    xcat_ref[halo:, :] = wide(xr_ref, slice(None))
    u = cb_ref[...]
    for k in range(CONV_WIDTH):
        d = CONV_WIDTH - 1 - k
        u = u + cw_ref[k:k + 1, :] * xcat_ref[pl.ds(halo - 8 * d, ROW_TILE), :]

    ub = u.astype(BF16)

    def gate(w_ref, b_ref):
        lo = jnp.dot(ub[:, :half], w_ref[0], preferred_element_type=F32)
        hi = jnp.dot(ub[:, half:], w_ref[1], preferred_element_type=F32)
        z = jnp.concatenate([lo, hi], axis=1) + b_ref[...]
        return 0.5 * jnp.tanh(0.5 * z) + 0.5

    gate_x = gate(wx_ref, bx_ref)
    gate_a = gate(wa_ref, ba_ref)
    neg_lam = -lam_ref[...]
    softplus = jnp.maximum(neg_lam, 0.0) + jnp.log1p(jnp.exp(-jnp.abs(neg_lam)))
    log_a = -LRU_C * gate_a * softplus
    a = jnp.exp(log_a)
    one_minus_a2 = -jnp.tanh(log_a) * (a * a + 1.0)
    root = jnp.where(one_minus_a2 > 0.0, one_minus_a2 * lax.rsqrt(one_minus_a2), 0.0)
    a_ref[...] = a
    b_ref[...] = u * gate_x * root

    def advance(j, carry):
        h_loc, prod = carry
        rows = pl.ds(pl.multiple_of(j * 8, 8), 8)
        a_j = a_ref[rows, :]
        h_loc = a_j * h_loc + b_ref[rows, :]
        prod = a_j * prod
        h_ref[rows, :] = h_loc
        a_ref[rows, :] = prod
        return h_loc, prod

    seg_end, seg_prod = lax.fori_loop(
        0, SEG_LEN, advance,
        (jnp.zeros((8, D_RNN), F32), jnp.ones((8, D_RNN), F32)), unroll=8)

    entering = [carry_ref[0:1, :]]
    for s in range(8):
        entering.append(seg_end[s:s + 1, :] + seg_prod[s:s + 1, :] * entering[s])
    carry_ref[0:1, :] = entering[8]
    state_in = jnp.tile(jnp.concatenate(entering[:8], axis=0), (SEG_LEN, 1))

    g = wide(gr_ref, slice(None))
    g_half = 0.5 * g
    inner = g * (math.sqrt(2.0 / math.pi)
                 + (0.044715 * math.sqrt(2.0 / math.pi)) * (g * g))
    gelu = g_half + g_half * jnp.tanh(inner)
    out = (h_ref[...] + a_ref[...] * state_in) * gelu

    for t in range(n_slabs):
        nat_ref[t] = out[:, t * LANES:(t + 1) * LANES]
        for s in range(8):
            o_ref[s * SEG_LEN:(s + 1) * SEG_LEN, t * LANES:(t + 1) * LANES] = (
                nat_ref[t, pl.ds(s, SEG_LEN, stride=8), :])


def _rglru_call(xr, gr, conv_w, conv_b, wx, wa, b_x, b_a, lam):
    n_tiles, n_slabs = xr.shape[0], xr.shape[1]
    tile = pl.BlockSpec((1, n_slabs, ROW_TILE, LANES), lambda i: (i, 0, 0, 0))
    last_rows = 8 * CONV_WIDTH
    prev = pl.BlockSpec(
        (1, n_slabs, last_rows, LANES),
        lambda i: (jnp.maximum(i - 1, 0), 0, ROW_TILE // last_rows - 1, 0))
    vec = _resident((1, D_RNN))
    return pl.pallas_call(
        _rglru_kernel,
        out_shape=jax.ShapeDtypeStruct((n_tiles * ROW_TILE, D_RNN), F32),
        grid=(n_tiles,),
        in_specs=[tile, prev, tile, _resident(conv_w.shape), vec,
                  _resident(wx.shape), _resident(wa.shape), vec, vec, vec],
        out_specs=pl.BlockSpec((ROW_TILE, D_RNN), lambda i: (i, 0)),
        scratch_shapes=[pltpu.VMEM((ROW_TILE + 8 * (CONV_WIDTH - 1), D_RNN), F32),
                        pltpu.VMEM((ROW_TILE, D_RNN), F32),
                        pltpu.VMEM((ROW_TILE, D_RNN), F32),
                        pltpu.VMEM((ROW_TILE, D_RNN), F32),
                        pltpu.VMEM((n_slabs, ROW_TILE, LANES), F32),
                        pltpu.VMEM((8, D_RNN), F32)],
        compiler_params=pltpu.CompilerParams(
            dimension_semantics=("arbitrary",), vmem_limit_bytes=32 * MIB),
        name="rglru",
    )(xr, xr, gr, conv_w, conv_b, wx, wa, b_x, b_a, lam)


def _outproj_ffn_kernel(x_ref, attn_ref, rec_ref, mod_ref, g_ref, *refs):
    o_ref, acc_ref = refs[3:5]
    wmix_ref, win_ref, wout_ref = _load_weights_once(refs[:3], refs[5:])
    y = jnp.dot(attn_ref[...], wmix_ref[:D_ATTN, :], preferred_element_type=F32)
    y = y + jnp.dot(rec_ref[...].astype(BF16), wmix_ref[D_ATTN:, :],
                    preferred_element_type=F32)
    x2 = _post_residual(x_ref[...], y, mod_ref, g_ref, 1, 1.0)
    o_ref[...] = _ffn(2, x2, mod_ref, g_ref, win_ref, wout_ref, acc_ref)


def _outproj_ffn_call(x, attn, rec, mod9, g, w_mix, w_in, w_out):
    row = pl.BlockSpec((OUT_ROW_TILE, D_MODEL), lambda i: (i, 0))
    half = pl.BlockSpec((OUT_ROW_TILE, D_ATTN), lambda i: (i, 0))
    weights = (w_mix, w_in, w_out)
    return pl.pallas_call(
        _outproj_ffn_kernel,
        out_shape=jax.ShapeDtypeStruct(x.shape, F32),
        grid=(x.shape[0] // OUT_ROW_TILE,),
        in_specs=[row, half, half, _resident(mod9.shape), _resident(g.shape)]
                 + [pl.BlockSpec(memory_space=pl.ANY)] * len(weights),
        out_specs=row,
        scratch_shapes=[pltpu.VMEM((OUT_ROW_TILE, D_MODEL), F32)]
                       + _weight_scratch(weights),
        compiler_params=pltpu.CompilerParams(
            dimension_semantics=("arbitrary",), vmem_limit_bytes=58 * MIB),
        name="outproj_ffn",
    )(x, attn, rec, mod9, g, w_mix, w_in, w_out)


def _blockdiag_halves(w):
    per_half = (D_RNN // 2) // LRU_BLOCK
    eye = jnp.eye(per_half, dtype=w.dtype)
    wh = w.reshape(2, per_half, LRU_BLOCK, LRU_BLOCK)
    dense = jnp.einsum('hgij,gk->hgikj', wh, eye)
    return dense.reshape(2, D_RNN // 2, D_RNN // 2).astype(BF16)


def kernel(x, c, w_ada, b_ada, norm_gain, ffn1_w_in, ffn1_w_out, mix_w_in, conv_w, conv_b,
           lru_w_x, lru_b_x, lru_w_a, lru_b_a, lru_lambda, mix_w_out, ffn2_w_in, ffn2_w_out):
    batch, seq, d = x.shape
    assert (batch, seq, d) == (1, SEQ, D_MODEL) and w_ada.shape[0] == 1
    xs = x.reshape(seq, d)
    g = norm_gain[0]

    mod = _mod_call(c.reshape(d, 1), w_ada[0], b_ada)
    mod9 = mod.reshape(9, d)

    x1, *qkv, xr, gr = _ffn_inproj_call(
        xs, mod9, g, ffn1_w_in[0], ffn1_w_out[0], mix_w_in[0])
    attn = _attn_call(*qkv)
    rec = _rglru_call(xr, gr, conv_w[0], conv_b, _blockdiag_halves(lru_w_x[0]),
                      _blockdiag_halves(lru_w_a[0]), lru_b_x, lru_b_a, lru_lambda)
    x3 = _outproj_ffn_call(x1, attn, rec, mod9, g, mix_w_out[0],
                           ffn2_w_in[0], ffn2_w_out[0])
    return x3.reshape(batch, seq, d)
```

```python
import math

import jax
import jax.numpy as jnp
from jax import lax
from jax.experimental import pallas as pl
from jax.experimental.pallas import tpu as pltpu

F32 = jnp.float32
BF16 = jnp.bfloat16

D_MODEL = 1024
SEQ = 16384
HEAD_DIM = 64
D_ATTN = 512
D_RNN = 512
LRU_BLOCK = 64
N_BACK = 128
ATTN_BLOCK = 128
CONV_WIDTH = 4
LRU_C = 8.0
D_FF = 2816
FFN_RES_WEIGHT = 0.5
RMS_EPS = 1e-6

MIB = 1024 * 1024
SUBLANES = 8
ROW_TILE = 512
SEG_LEN = ROW_TILE // SUBLANES
LRU_PARTS = 4
FF_CHUNK = 256
ATTN_CHUNK = 2048
LANES = 128
STAGE_ROWS, STAGE_COLS = 256, 512
STAGE_SLOTS = 8
PAIR = 2 * HEAD_DIM
MOD_TILE = 1152
ATTN_GROUP = 4


def _resident(shape):
    return pl.BlockSpec(shape, lambda *_: (0,) * len(shape),
                        pipeline_mode=pl.Buffered(1))


def _weight_scratch(weights):
    for w in weights:
        assert w.shape[0] % STAGE_ROWS == 0 and w.shape[1] % STAGE_COLS == 0, w.shape
    return ([pltpu.VMEM(w.shape, BF16) for w in weights]
            + [pltpu.VMEM((STAGE_SLOTS, STAGE_ROWS, STAGE_COLS), F32),
               pltpu.SemaphoreType.DMA((STAGE_SLOTS,))])


def _load_weights_once(hbm_refs, scratch_refs):
    *dst_refs, stage_ref, sem = scratch_refs
    blocks = [(hbm, dst, r, c)
              for hbm, dst in zip(hbm_refs, dst_refs)
              for r in range(0, dst.shape[0], STAGE_ROWS)
              for c in range(0, dst.shape[1], STAGE_COLS)]

    def copy(i):
        hbm, _, r, c = blocks[i]
        slot = i % STAGE_SLOTS
        return pltpu.make_async_copy(
            hbm.at[pl.ds(r, STAGE_ROWS), pl.ds(c, STAGE_COLS)],
            stage_ref.at[slot], sem.at[slot])

    @pl.when(pl.program_id(0) == 0)
    def _():
        for i in range(min(STAGE_SLOTS, len(blocks))):
            copy(i).start()
        for i, (_, dst, r, c) in enumerate(blocks):
            copy(i).wait()
            dst[r:r + STAGE_ROWS, c:c + STAGE_COLS] = (
                stage_ref[i % STAGE_SLOTS].astype(BF16))
            if i + STAGE_SLOTS < len(blocks):
                copy(i + STAGE_SLOTS).start()

    return dst_refs


def _rms(v):
    return v * lax.rsqrt(jnp.mean(v * v, axis=-1, keepdims=True) + RMS_EPS)


def _pre(x, mod_ref, g_ref, sub):
    shift = mod_ref[3 * sub:3 * sub + 1, :]
    scale = mod_ref[3 * sub + 1:3 * sub + 2, :]
    gain = g_ref[2 * sub:2 * sub + 1, :] * (1.0 + scale)
    return _rms(x) * gain + shift


def _post_residual(x, y, mod_ref, g_ref, sub, weight):
    gate = mod_ref[3 * sub + 2:3 * sub + 3, :]
    gain = weight * (1.0 + gate) * g_ref[2 * sub + 1:2 * sub + 2, :]
    return x + _rms(y) * gain


def _mod_kernel(c_ref, w_ref, b_ref, o_ref):
    c = c_ref[...]
    s = c * jax.nn.sigmoid(c)
    o_ref[...] = jnp.sum(s * w_ref[...], axis=0, keepdims=True) + b_ref[...]


def _mod_call(c_col, w_ada, b_ada):
    n = w_ada.shape[1]
    return pl.pallas_call(
        _mod_kernel,
        out_shape=jax.ShapeDtypeStruct((1, n), F32),
        grid=(n // MOD_TILE,),
        in_specs=[
            pl.BlockSpec((D_MODEL, 1), lambda j: (0, 0)),
            pl.BlockSpec((D_MODEL, MOD_TILE), lambda j: (0, j)),
            pl.BlockSpec((1, MOD_TILE), lambda j: (0, j)),
        ],
        out_specs=pl.BlockSpec((1, MOD_TILE), lambda j: (0, j)),
        compiler_params=pltpu.CompilerParams(
            dimension_semantics=("arbitrary",), vmem_limit_bytes=24 * MIB),
        name="mod",
    )(c_col, w_ada, b_ada)


def _ffn(sub, x, mod_ref, g_ref, win_ref, wout_ref, acc_ref, side_work=()):
    side_work = iter(side_work)
    h = _pre(x, mod_ref, g_ref, sub).astype(BF16)
    for lo in range(0, D_FF, FF_CHUNK):
        hi = min(lo + FF_CHUNK, D_FF)
        hg = jnp.dot(h, win_ref[:, lo:hi], preferred_element_type=F32)
        hu = jnp.dot(h, win_ref[:, D_FF + lo:D_FF + hi], preferred_element_type=F32)
        a = (hg * jax.nn.sigmoid(hg) * hu).astype(BF16)
        part = jnp.dot(a, wout_ref[lo:hi, :], preferred_element_type=F32)
        if lo == 0:
            acc_ref[...] = part
        else:
            acc_ref[...] += part
        for _ in range(2):
            next(side_work, None)
    for _ in side_work:
        pass
    return _post_residual(x, acc_ref[...], mod_ref, g_ref, sub, FFN_RES_WEIGHT)


def _ffn_inproj_kernel(x_ref, mod_ref, g_ref, *refs):
    x1_ref, kv1, qkv4, qkv16, xg_ref, acc_ref, z_ref, z4_ref = refs[3:11]
    win_ref, wout_ref, wmix_ref = _load_weights_once(refs[:3], refs[11:])
    x1 = _ffn(0, x_ref[...], mod_ref, g_ref, win_ref, wout_ref, acc_ref)
    x1_ref[...] = x1
    h = _pre(x1, mod_ref, g_ref, 1).astype(BF16)

    def proj(j, width):
        return jnp.dot(h, wmix_ref[:, j:j + width], preferred_element_type=F32)

    for n in range(3):
        o1 = kv1.at[n - 1] if n else None
        o4, o16 = qkv4.at[n], qkv16.at[n]
        z = proj(n * D_ATTN, D_ATTN)
        if n == 0:
            z = z * (HEAD_DIM ** -0.5 * math.log2(math.e))
        for t in range(D_ATTN // LANES):
            cols = slice(t * LANES, (t + 1) * LANES)
            if o1 is not None:
                o1[:, cols] = z[:, cols].astype(BF16)
            z_ref[n, t] = z[:, cols]
            for r4 in range(4):
                z4 = z_ref[n, t, pl.ds(r4, ROW_TILE // 4, stride=4), :]
                o4[0, r4, :, cols] = z4.astype(BF16)
                z4_ref[n, t, r4] = z4
                for e in range(4):
                    z16 = z4_ref[n, t, r4, pl.ds(e, ROW_TILE // 16, stride=4), :]
                    o16[0, r4 + 4 * e, :, cols] = z16.astype(BF16)

    for n in range(2):
        z = proj(3 * D_ATTN + n * D_RNN, D_RNN)
        for t in range(D_RNN // LANES):
            for s in range(SUBLANES):
                xg_ref[n, 0, t, pl.ds(s, SEG_LEN, stride=SUBLANES), :] = z[
                    s * SEG_LEN:(s + 1) * SEG_LEN, t * LANES:(t + 1) * LANES]


def _ffn_inproj_call(x, mod9, g, w_in, w_out, w_mix):
    s = x.shape[0]
    row = pl.BlockSpec((ROW_TILE, D_MODEL), lambda i: (i, 0))
    nat = pl.BlockSpec((2, ROW_TILE, D_ATTN), lambda i: (0, i, 0))
    c4 = pl.BlockSpec((3, 1, 4, ROW_TILE // 4, D_ATTN), lambda i: (0, i, 0, 0, 0))
    per16 = ATTN_CHUNK // ROW_TILE
    c16 = pl.BlockSpec((3, 1, 16, ROW_TILE // 16, D_ATTN),
                       lambda i: (0, i // per16, 0, i % per16, 0))
    s_nat = jax.ShapeDtypeStruct((2, s, D_ATTN), BF16)
    s_c4 = jax.ShapeDtypeStruct((3, s // ROW_TILE, 4, ROW_TILE // 4, D_ATTN), BF16)
    s_c16 = jax.ShapeDtypeStruct((3, s // ATTN_CHUNK, 16, ATTN_BLOCK, D_ATTN), BF16)
    seg = pl.BlockSpec((2, 1, D_RNN // LANES, ROW_TILE, LANES),
                       lambda i: (0, i, 0, 0, 0))
    s_seg = jax.ShapeDtypeStruct((2, s // ROW_TILE, D_RNN // LANES, ROW_TILE, LANES), F32)
    weights = (w_in, w_out, w_mix)
    return pl.pallas_call(
        _ffn_inproj_kernel,
        out_shape=[jax.ShapeDtypeStruct(x.shape, F32), s_nat, s_c4, s_c16, s_seg],
        grid=(s // ROW_TILE,),
        in_specs=[row, _resident(mod9.shape), _resident(g.shape)]
                 + [pl.BlockSpec(memory_space=pl.ANY)] * len(weights),
        out_specs=[row, nat, c4, c16, seg],
        scratch_shapes=[pltpu.VMEM((ROW_TILE, D_MODEL), F32),
                        pltpu.VMEM((3, D_ATTN // LANES, ROW_TILE, LANES), F32),
                        pltpu.VMEM((3, D_ATTN // LANES, 4, ROW_TILE // 4, LANES), F32)]
                       + _weight_scratch(weights),
        compiler_params=pltpu.CompilerParams(
            dimension_semantics=("arbitrary",), vmem_limit_bytes=58 * MIB),
        name="ffn_inproj",
    )(x, mod9, g, w_in, w_out, w_mix)


def _attn_kernel(k1, v1, k1p, v1p, q4, k4, v4, k4p, v4p,
                 q16, k16, v16, k16p, v16p, o_ref,
                 kc1, vc1, kc4, vc4, kc16, vc16,
                 bias_ref, s_ref, p_ref, max_ref, acc_ref, den_ref, nat_ref):
    chunk = pl.program_id(0)
    nblk = ATTN_CHUNK // ATTN_BLOCK
    sub_rows = 4 * ATTN_BLOCK
    quarter = ATTN_BLOCK // 4
    lane = lax.broadcasted_iota(jnp.int32, (1, PAIR), 1)
    head_mask = [(lane < HEAD_DIM).astype(BF16), (lane >= HEAD_DIM).astype(BF16)]

    @pl.when((chunk == 0) & (pl.program_id(1) == 0))
    def _():
        for vc in (vc1, vc4, vc16):
            for h in range(2):
                vc[h, :, :, PAIR:] = jnp.broadcast_to(
                    head_mask[h], vc.shape[1:3] + (PAIR,))
        row = lax.broadcasted_iota(jnp.int32, (ATTN_BLOCK, 2 * ATTN_BLOCK), 0)
        col = lax.broadcasted_iota(jnp.int32, (ATTN_BLOCK, 2 * ATTN_BLOCK), 1)
        row4 = (row % quarter) * 4 + row // quarter
        for kind, r in enumerate((row, row4)):
            band = (col >= r) & (col <= r + N_BACK)
            bias_ref[2 * kind] = jnp.where(band, 0.0, -jnp.inf)
            bias_ref[2 * kind + 1] = jnp.where(band & (col >= ATTN_BLOCK), 0.0, -jnp.inf)

    def put(kc, vc, cls, rows, k, v):
        kc[cls, rows] = k
        for h in range(2):
            vc[h, cls, rows, :PAIR] = v * head_mask[h]

    prev = slice(0, ATTN_BLOCK)
    put(kc1, vc1, 0, prev, k1p[...], v1p[...])
    put(kc1, vc1, 0, slice(ATTN_BLOCK, None), k1[...], v1[...])
    for r in range(4):
        put(kc4, vc4, r, prev, k4p[0, r], v4p[0, r])
        for i in range(4):
            rows = slice((i + 1) * ATTN_BLOCK, (i + 2) * ATTN_BLOCK)
            put(kc4, vc4, r, rows, k4[i, r], v4[i, r])
    for r in range(16):
        put(kc16, vc16, r, prev, k16p[0, r], v16p[0, r])
        put(kc16, vc16, r, slice(ATTN_BLOCK, None), k16[0, r], v16[0, r])

    max_ref[...] = jnp.full(max_ref.shape, -jnp.inf, F32)
    acc_ref[...] = jnp.zeros(acc_ref.shape, F32)
    den_ref[...] = jnp.zeros(den_ref.shape, F32)

    def window(j):
        return pl.ds(j * ATTN_BLOCK, 2 * ATTN_BLOCK)

    def dilation1(j):
        i, jj = divmod(j, 4)
        rows = slice(jj * quarter, (jj + 1) * quarter)
        q = jnp.concatenate([q4[i, r4, rows, :] for r4 in range(4)], axis=0)
        pieces = [(pl.ds(i * sub_rows + r4 * ATTN_BLOCK + jj * quarter, quarter),
                   slice(r4 * quarter, (r4 + 1) * quarter)) for r4 in range(4)]
        return (q, lambda: kc1[0, window(j), :], lambda h: vc1[h, 0, window(j), :],
                True if j > 0 else chunk > 0, 1, pieces)

    def dilation4(t):
        i, r4 = divmod(t, 4)
        pieces = [(pl.ds(i * sub_rows + r4 * ATTN_BLOCK, ATTN_BLOCK),
                   slice(0, ATTN_BLOCK))]
        return (q4[i, r4], lambda: kc4[r4, window(i), :],
                lambda h: vc4[h, r4, window(i), :],
                True if i > 0 else chunk > 0, 0, pieces)

    def dilation16(r):
        e, r4 = divmod(r, 4)
        pieces = [(pl.ds(i * sub_rows + r4 * ATTN_BLOCK + e, quarter, stride=4),
                   slice(i * quarter, (i + 1) * quarter)) for i in range(4)]
        return (q16[0, r], lambda: kc16[r], lambda h: vc16[h, r],
                chunk > 0, 0, pieces)

    makers = [dilation4] * nblk + [dilation16] * nblk + [dilation1] * nblk
    n_tiles = len(makers)

    def scores(t):
        q, keys, _, has_prev, kind, pieces = makers[t](t % nblk)
        no_prev = 0 if has_prev is True else jnp.where(has_prev, 0, 1)
        bias = bias_ref[2 * kind + no_prev]
        kk = keys()
        for h in range(2):
            s = lax.dot_general(q * head_mask[h], kk, (((1,), (1,)), ((), ())),
                                preferred_element_type=F32) + bias
            s_ref[t, h] = s
            m = jnp.broadcast_to(jnp.max(s, axis=-1, keepdims=True),
                                 (ATTN_BLOCK, PAIR))
            for canon, rows in pieces:
                max_ref[h, canon, :] = jnp.maximum(max_ref[h, canon, :], m[rows])

    def softmax(t):
        pieces = makers[t](t % nblk)[5]
        for h in range(2):
            m = jnp.concatenate([max_ref[h, canon, :] for canon, _ in pieces], axis=0)
            p = jnp.exp2(s_ref[t, h] - jnp.concatenate([m, m], axis=1))
            p_ref[t % nblk, h] = p.astype(BF16)

    def values(t):
        _, _, vals, _, _, pieces = makers[t](t % nblk)
        pv = (jnp.dot(p_ref[t % nblk, 0], vals(0), preferred_element_type=F32)
              + jnp.dot(p_ref[t % nblk, 1], vals(1), preferred_element_type=F32))
        for canon, rows in pieces:
            acc_ref[canon, :] += pv[rows, :PAIR]
            den_ref[canon, :] += pv[rows, PAIR:]

    def pipeline(stages):
        n_groups = n_tiles // ATTN_GROUP
        for i in range(n_groups + len(stages) - 1):
            for lag in reversed(range(len(stages))):
                if 0 <= i - lag < n_groups:
                    for j in range(ATTN_GROUP):
                        stages[lag]((i - lag) * ATTN_GROUP + j)

    pipeline([scores])
    pipeline([softmax, values])

    for i in range(ATTN_CHUNK // sub_rows):
        for r4 in range(4):
            canon = pl.ds(i * sub_rows + r4 * ATTN_BLOCK, ATTN_BLOCK)
            nat_ref[pl.ds(i * sub_rows + r4, ATTN_BLOCK, stride=4), :] = (
                acc_ref[canon, :] / den_ref[canon, :])
    o_ref[...] = nat_ref[...].astype(BF16)


def _attn_call(kv1, qkv4, qkv16):
    s = kv1.shape[1]
    n_pairs = D_ATTN // PAIR
    blocks_per_chunk = ATTN_CHUNK // ATTN_BLOCK
    sub_per_chunk = ATTN_CHUNK // (4 * ATTN_BLOCK)

    def nat(n):
        return pl.BlockSpec((None, ATTN_CHUNK, PAIR), lambda c, p: (n, c, p))

    def nat_prev(n):
        return pl.BlockSpec(
            (None, ATTN_BLOCK, PAIR),
            lambda c, p: (n, jnp.maximum(c * blocks_per_chunk - 1, 0), p))

    def c4(n):
        return pl.BlockSpec((None, sub_per_chunk, 4, ATTN_BLOCK, PAIR),
                            lambda c, p: (n, c, 0, 0, p))

    def c4_prev(n):
        return pl.BlockSpec(
            (None, 1, 4, ATTN_BLOCK, PAIR),
            lambda c, p: (n, jnp.maximum(c * sub_per_chunk - 1, 0), 0, 0, p))

    def c16(n):
        return pl.BlockSpec((None, 1, 16, ATTN_BLOCK, PAIR),
                            lambda c, p: (n, c, 0, 0, p))

    def c16_prev(n):
        return pl.BlockSpec((None, 1, 16, ATTN_BLOCK, PAIR),
                            lambda c, p: (n, jnp.maximum(c - 1, 0), 0, 0, p))

    q, k, v = 0, 1, 2
    kcat = lambda classes, blocks: pltpu.VMEM(
        (classes, (blocks + 1) * ATTN_BLOCK, PAIR), BF16)
    vcat = lambda classes, blocks: pltpu.VMEM(
        (2, classes, (blocks + 1) * ATTN_BLOCK, 2 * PAIR), BF16)
    rows = lambda: pltpu.VMEM((ATTN_CHUNK, PAIR), F32)
    return pl.pallas_call(
        _attn_kernel,
        out_shape=jax.ShapeDtypeStruct((s, D_ATTN), BF16),
        grid=(s // ATTN_CHUNK, n_pairs),
        in_specs=[nat(k - 1), nat(v - 1), nat_prev(k - 1), nat_prev(v - 1),
                  c4(q), c4(k), c4(v), c4_prev(k), c4_prev(v),
                  c16(q), c16(k), c16(v), c16_prev(k), c16_prev(v)],
        out_specs=pl.BlockSpec((ATTN_CHUNK, PAIR), lambda c, p: (c, p)),
        scratch_shapes=[
            kcat(1, 16), vcat(1, 16), kcat(4, 4), vcat(4, 4), kcat(16, 1), vcat(16, 1),
            pltpu.VMEM((4, ATTN_BLOCK, 2 * ATTN_BLOCK), F32),
            pltpu.VMEM((3 * blocks_per_chunk, 2, ATTN_BLOCK, 2 * ATTN_BLOCK), F32),
            pltpu.VMEM((blocks_per_chunk, 2, ATTN_BLOCK, 2 * ATTN_BLOCK), BF16),
            pltpu.VMEM((2, ATTN_CHUNK, PAIR), F32),
            rows(), rows(), rows()],
        compiler_params=pltpu.CompilerParams(
            dimension_semantics=("arbitrary", "arbitrary"),
            vmem_limit_bytes=56 * MIB),
        name="attn",
    )(kv1, kv1, kv1, kv1, *[qkv4] * 5, *[qkv16] * 5)


def _rglru_phases(xr_ref, prev_rows, gr_ref, params, rec_ref, scratch):
    cw_ref, cb_ref, wx_ref, wa_ref, bx_ref, ba_ref, lam_ref = params
    xcat_ref, a_ref, b_ref, h_ref, nat_ref, carry_ref = scratch
    half = D_RNN // 2
    n_slabs = D_RNN // LANES
    halo = SUBLANES * (CONV_WIDTH - 1)
    part = ROW_TILE // LRU_PARTS

    def wide(ref, rows):
        return jnp.concatenate([ref[0, t, rows, :] for t in range(n_slabs)], axis=1)

    sub = lax.broadcasted_iota(jnp.int32, (SUBLANES, D_RNN), 0)
    for d in range(1, CONV_WIDTH):
        cur = wide(xr_ref, pl.ds(SUBLANES * (SEG_LEN - d), 8))
        xcat_ref[pl.ds(halo - SUBLANES * d, SUBLANES), :] = jnp.where(
            sub == 0, pltpu.roll(prev_rows(d), 1, 0), pltpu.roll(cur, 1, 0))
    xcat_ref[halo:, :] = wide(xr_ref, slice(None))
    yield

    neg_lam = -lam_ref[...]
    softplus = jnp.maximum(neg_lam, 0.0) + jnp.log1p(jnp.exp(-jnp.abs(neg_lam)))
    seg_end = jnp.zeros((SUBLANES, D_RNN), F32)
    seg_prod = jnp.ones((SUBLANES, D_RNN), F32)
    for p in range(LRU_PARTS):
        rows = pl.ds(p * part, part)
        u = cb_ref[...]
        for k in range(CONV_WIDTH):
            d = CONV_WIDTH - 1 - k
            u = u + cw_ref[k:k + 1, :] * xcat_ref[pl.ds(halo - SUBLANES * d + p * part, part), :]
        ub = u.astype(BF16)

        def gate(w_ref, b_ref):
            lo = jnp.dot(ub[:, :half], w_ref[0], preferred_element_type=F32)
            hi = jnp.dot(ub[:, half:], w_ref[1], preferred_element_type=F32)
            z = jnp.concatenate([lo, hi], axis=1) + b_ref[...]
            return 0.5 * jnp.tanh(0.5 * z) + 0.5

        gate_x = gate(wx_ref, bx_ref)
        gate_a = gate(wa_ref, ba_ref)
        log_a = -LRU_C * gate_a * softplus
        a = jnp.exp(log_a)
        one_minus_a2 = -jnp.tanh(log_a) * (a * a + 1.0)
        root = jnp.where(one_minus_a2 > 0.0,
                         one_minus_a2 * lax.rsqrt(one_minus_a2), 0.0)
        a_ref[rows, :] = a
        b_ref[rows, :] = u * gate_x * root
        yield

        for j in range(p * part // SUBLANES, (p + 1) * part // SUBLANES):
            step_rows = pl.ds(j * SUBLANES, SUBLANES)
            a_j = a_ref[step_rows, :]
            seg_end = a_j * seg_end + b_ref[step_rows, :]
            seg_prod = a_j * seg_prod
            h_ref[step_rows, :] = seg_end
            a_ref[step_rows, :] = seg_prod
        yield

    entering = [carry_ref[0:1, :]]
    for s in range(SUBLANES):
        entering.append(seg_end[s:s + 1, :] + seg_prod[s:s + 1, :] * entering[s])
    carry_ref[0:1, :] = entering[SUBLANES]
    state_in = jnp.tile(jnp.concatenate(entering[:SUBLANES], axis=0),
                        (part // SUBLANES, 1))

    for p in range(LRU_PARTS):
        rows = pl.ds(p * part, part)
        g = wide(gr_ref, rows)
        g_half = 0.5 * g
        inner = g * (math.sqrt(2.0 / math.pi)
                     + (0.044715 * math.sqrt(2.0 / math.pi)) * (g * g))
        gelu = g_half + g_half * jnp.tanh(inner)
        out = (h_ref[rows, :] + a_ref[rows, :] * state_in) * gelu
        for t in range(n_slabs):
            nat_ref[t, rows, :] = out[:, t * LANES:(t + 1) * LANES]
        yield

    for t in range(n_slabs):
        for s in range(SUBLANES):
            rec_ref[s * SEG_LEN:(s + 1) * SEG_LEN, t * LANES:(t + 1) * LANES] = (
                nat_ref[t, pl.ds(s, SEG_LEN, stride=SUBLANES), :].astype(BF16))
        yield


def _mixer_ffn_kernel(x_ref, attn_ref, xr0_ref, gr0_ref, xrn_ref, xpn_ref, grn_ref,
                      mod_ref, g_ref, *refs):
    lru_params = refs[:7]
    o_ref, acc_ref, rec_ref = refs[10:13]
    lru_scratch = refs[13:19]
    wmix_ref, win_ref, wout_ref = _load_weights_once(refs[7:10], refs[19:])
    step = pl.program_id(0)
    n_slabs = D_RNN // LANES

    @pl.when(step == 0)
    def _():
        lru_scratch[-1][...] = jnp.zeros_like(lru_scratch[-1])
        for _ in _rglru_phases(xr0_ref, lambda d: jnp.zeros((SUBLANES, D_RNN), F32), gr0_ref,
                               lru_params, rec_ref.at[0], lru_scratch):
            pass

    def prev_rows(d):
        return jnp.concatenate(
            [xpn_ref[0, t, pl.ds(SUBLANES * (CONV_WIDTH - d), SUBLANES), :] for t in range(n_slabs)],
            axis=1)

    cur = step % 2
    next_rec = _rglru_phases(xrn_ref, prev_rows, grn_ref, lru_params,
                             rec_ref.at[1 - cur], lru_scratch)
    y = jnp.dot(attn_ref[...], wmix_ref[:D_ATTN, :], preferred_element_type=F32)
    y = y + jnp.dot(rec_ref[cur], wmix_ref[D_ATTN:, :], preferred_element_type=F32)
    x2 = _post_residual(x_ref[...], y, mod_ref, g_ref, 1, 1.0)
    o_ref[...] = _ffn(2, x2, mod_ref, g_ref, win_ref, wout_ref, acc_ref,
                      side_work=next_rec)


def _mixer_ffn_call(x, attn, xg, mod9, g, lru_params, w_mix, w_in, w_out):
    n_tiles, n_slabs = xg.shape[1], xg.shape[2]
    row = pl.BlockSpec((ROW_TILE, D_MODEL), lambda i: (i, 0))
    half = pl.BlockSpec((ROW_TILE, D_ATTN), lambda i: (i, 0))
    nxt = lambda i: jnp.minimum(i + 1, n_tiles - 1)

    def first_tile(n):
        return pl.BlockSpec((None, 1, n_slabs, ROW_TILE, LANES),
                            lambda i: (n, 0, 0, 0, 0), pipeline_mode=pl.Buffered(1))

    def next_tile(n):
        return pl.BlockSpec((None, 1, n_slabs, ROW_TILE, LANES),
                            lambda i: (n, nxt(i), 0, 0, 0))

    last_rows = SUBLANES * CONV_WIDTH
    before_next = pl.BlockSpec(
        (None, 1, n_slabs, last_rows, LANES),
        lambda i: (0, jnp.maximum(nxt(i) - 1, 0), 0, ROW_TILE // last_rows - 1, 0))
    weights = (w_mix, w_in, w_out)
    return pl.pallas_call(
        _mixer_ffn_kernel,
        out_shape=jax.ShapeDtypeStruct(x.shape, F32),
        grid=(n_tiles,),
        in_specs=[row, half, first_tile(0), first_tile(1), next_tile(0), before_next,
                  next_tile(1), _resident(mod9.shape), _resident(g.shape)]
                 + [_resident(p.shape) for p in lru_params]
                 + [pl.BlockSpec(memory_space=pl.ANY)] * len(weights),
        out_specs=row,
        scratch_shapes=[pltpu.VMEM((ROW_TILE, D_MODEL), F32),
                        pltpu.VMEM((2, ROW_TILE, D_RNN), BF16),
                        pltpu.VMEM((ROW_TILE + SUBLANES * (CONV_WIDTH - 1), D_RNN), F32),
                        pltpu.VMEM((ROW_TILE, D_RNN), F32),
                        pltpu.VMEM((ROW_TILE, D_RNN), F32),
                        pltpu.VMEM((ROW_TILE, D_RNN), F32),
                        pltpu.VMEM((n_slabs, ROW_TILE, LANES), F32),
                        pltpu.VMEM((SUBLANES, D_RNN), F32)]
                       + _weight_scratch(weights),
        compiler_params=pltpu.CompilerParams(
            dimension_semantics=("arbitrary",), vmem_limit_bytes=56 * MIB),
        name="mixer_ffn",
    )(x, attn, xg, xg, xg, xg, xg, mod9, g, *lru_params, w_mix, w_in, w_out)


def _blockdiag_halves(w):
    per_half = (D_RNN // 2) // LRU_BLOCK
    eye = jnp.eye(per_half, dtype=w.dtype)
    wh = w.reshape(2, per_half, LRU_BLOCK, LRU_BLOCK)
    dense = jnp.einsum('hgij,gk->hgikj', wh, eye)
    return dense.reshape(2, D_RNN // 2, D_RNN // 2).astype(BF16)


def kernel(x, c, w_ada, b_ada, norm_gain, ffn1_w_in, ffn1_w_out, mix_w_in, conv_w, conv_b,
           lru_w_x, lru_b_x, lru_w_a, lru_b_a, lru_lambda, mix_w_out, ffn2_w_in, ffn2_w_out):
    batch, seq, d = x.shape
    assert (batch, seq, d) == (1, SEQ, D_MODEL) and w_ada.shape[0] == 1
    xs = x.reshape(seq, d)
    g = norm_gain[0]

    mod = _mod_call(c.reshape(d, 1), w_ada[0], b_ada)
    mod9 = mod.reshape(9, d)

    x1, kv1, qkv4, qkv16, xg = _ffn_inproj_call(
        xs, mod9, g, ffn1_w_in[0], ffn1_w_out[0], mix_w_in[0])
    attn = _attn_call(kv1, qkv4, qkv16)
    lru_params = (conv_w[0], conv_b, _blockdiag_halves(lru_w_x[0]),
                  _blockdiag_halves(lru_w_a[0]), lru_b_x, lru_b_a, lru_lambda)
    x3 = _mixer_ffn_call(x1, attn, xg, mod9, g, lru_params, mix_w_out[0],
                         ffn2_w_in[0], ffn2_w_out[0])
    return x3.reshape(batch, seq, d)
```

```python
import math

import jax
import jax.numpy as jnp
from jax import lax
from jax.experimental import pallas as pl
from jax.experimental.pallas import tpu as pltpu

F32 = jnp.float32
BF16 = jnp.bfloat16

D_MODEL = 1024
SEQ = 16384
HEAD_DIM = 64
D_ATTN = 512
D_RNN = 512
LRU_BLOCK = 64
N_BACK = 128
ATTN_BLOCK = 128
CONV_WIDTH = 4
LRU_C = 8.0
D_FF = 2816
FFN_RES_WEIGHT = 0.5
RMS_EPS = 1e-6

MIB = 1024 * 1024
SUBLANES = 8
ROW_TILE = 512
SEG_LEN = ROW_TILE // SUBLANES
LRU_PARTS = 4
FF_CHUNK = 256
ATTN_CHUNK = 2048
LANES = 128
STAGE_ROWS, STAGE_COLS = 256, 512
STAGE_SLOTS = 12
PAIR = 2 * HEAD_DIM
MOD_TILE = 2304
ATTN_GROUP = 4


def _resident(shape):
    return pl.BlockSpec(shape, lambda *_: (0,) * len(shape),
                        pipeline_mode=pl.Buffered(1))


def _weight_scratch(weights):
    for w in weights:
        assert w.shape[0] % STAGE_ROWS == 0 and w.shape[1] % STAGE_COLS == 0, w.shape
    return ([pltpu.VMEM(w.shape, BF16) for w in weights]
            + [pltpu.VMEM((STAGE_SLOTS, STAGE_ROWS, STAGE_COLS), F32),
               pltpu.SemaphoreType.DMA((STAGE_SLOTS,))])


def _load_weights_once(hbm_refs, scratch_refs):
    *dst_refs, stage_ref, sem = scratch_refs
    blocks = [(hbm, dst, r, c)
              for hbm, dst in zip(hbm_refs, dst_refs)
              for r in range(0, dst.shape[0], STAGE_ROWS)
              for c in range(0, dst.shape[1], STAGE_COLS)]

    def copy(i):
        hbm, _, r, c = blocks[i]
        slot = i % STAGE_SLOTS
        return pltpu.make_async_copy(
            hbm.at[pl.ds(r, STAGE_ROWS), pl.ds(c, STAGE_COLS)],
            stage_ref.at[slot], sem.at[slot])

    @pl.when(pl.program_id(0) == 0)
    def _():
        for i in range(min(STAGE_SLOTS, len(blocks))):
            copy(i).start()
        for i, (_, dst, r, c) in enumerate(blocks):
            copy(i).wait()
            dst[r:r + STAGE_ROWS, c:c + STAGE_COLS] = (
                stage_ref[i % STAGE_SLOTS].astype(BF16))
            if i + STAGE_SLOTS < len(blocks):
                copy(i + STAGE_SLOTS).start()

    return dst_refs


def _rms(v):
    return v * lax.rsqrt(jnp.mean(v * v, axis=-1, keepdims=True) + RMS_EPS)


def _pre(x, mod_ref, g_ref, sub):
    shift = mod_ref[3 * sub:3 * sub + 1, :]
    scale = mod_ref[3 * sub + 1:3 * sub + 2, :]
    gain = g_ref[2 * sub:2 * sub + 1, :] * (1.0 + scale)
    return _rms(x) * gain + shift


def _post_residual(x, y, mod_ref, g_ref, sub, weight):
    gate = mod_ref[3 * sub + 2:3 * sub + 3, :]
    gain = weight * (1.0 + gate) * g_ref[2 * sub + 1:2 * sub + 2, :]
    return x + _rms(y) * gain


def _mod_kernel(c_ref, w_ref, b_ref, o_ref):
    c = c_ref[...]
    s = c * jax.nn.sigmoid(c)
    o_ref[...] = jnp.sum(s * w_ref[...], axis=0, keepdims=True) + b_ref[...]


def _mod_call(c_col, w_ada, b_ada):
    n = w_ada.shape[1]
    return pl.pallas_call(
        _mod_kernel,
        out_shape=jax.ShapeDtypeStruct((1, n), F32),
        grid=(n // MOD_TILE,),
        in_specs=[
            pl.BlockSpec((D_MODEL, 1), lambda j: (0, 0)),
            pl.BlockSpec((D_MODEL, MOD_TILE), lambda j: (0, j)),
            pl.BlockSpec((1, MOD_TILE), lambda j: (0, j)),
        ],
        out_specs=pl.BlockSpec((1, MOD_TILE), lambda j: (0, j)),
        compiler_params=pltpu.CompilerParams(
            dimension_semantics=("arbitrary",), vmem_limit_bytes=24 * MIB),
        name="mod",
    )(c_col, w_ada, b_ada)


def _ffn(sub, x, mod_ref, g_ref, win_ref, wout_ref, acc_ref, side_work=()):
    side_work = iter(side_work)
    h = _pre(x, mod_ref, g_ref, sub).astype(BF16)
    for lo in range(0, D_FF, FF_CHUNK):
        hi = min(lo + FF_CHUNK, D_FF)
        hg = jnp.dot(h, win_ref[:, lo:hi], preferred_element_type=F32)
        hu = jnp.dot(h, win_ref[:, D_FF + lo:D_FF + hi], preferred_element_type=F32)
        a = (hg * jax.nn.sigmoid(hg) * hu).astype(BF16)
        part = jnp.dot(a, wout_ref[lo:hi, :], preferred_element_type=F32)
        if lo == 0:
            acc_ref[...] = part
        else:
            acc_ref[...] += part
        for _ in range(2):
            next(side_work, None)
    for _ in side_work:
        pass
    return _post_residual(x, acc_ref[...], mod_ref, g_ref, sub, FFN_RES_WEIGHT)


def _ffn_inproj_kernel(x_ref, mod_ref, g_ref, *refs):
    x1_ref, q_ref, k_ref, v_ref, xr_ref, gr_ref, acc_ref = refs[3:10]
    win_ref, wout_ref, wmix_ref = _load_weights_once(refs[:3], refs[10:])
    x1 = _ffn(0, x_ref[...], mod_ref, g_ref, win_ref, wout_ref, acc_ref)
    x1_ref[...] = x1
    h = _pre(x1, mod_ref, g_ref, 1).astype(BF16)

    def proj(j, width):
        return jnp.dot(h, wmix_ref[:, j:j + width], preferred_element_type=F32)

    q_ref[...] = proj(0, D_ATTN) * (HEAD_DIM ** -0.5 * math.log2(math.e))
    k_ref[...] = proj(D_ATTN, D_ATTN)
    v_ref[...] = proj(2 * D_ATTN, D_ATTN)

    for n, dst in enumerate((xr_ref, gr_ref)):
        z = proj(3 * D_ATTN + n * D_RNN, D_RNN)
        for t in range(D_RNN // LANES):
            for s in range(SUBLANES):
                dst[0, t, pl.ds(s, SEG_LEN, stride=SUBLANES), :] = z[
                    s * SEG_LEN:(s + 1) * SEG_LEN, t * LANES:(t + 1) * LANES]


def _ffn_inproj_call(x, mod9, g, w_in, w_out, w_mix):
    s = x.shape[0]
    row = pl.BlockSpec((ROW_TILE, D_MODEL), lambda i: (i, 0))
    nat = pl.BlockSpec((ROW_TILE, D_ATTN), lambda i: (i, 0))
    s_nat = jax.ShapeDtypeStruct((s, D_ATTN), F32)
    seg = pl.BlockSpec((1, D_RNN // LANES, ROW_TILE, LANES), lambda i: (i, 0, 0, 0))
    s_seg = jax.ShapeDtypeStruct((s // ROW_TILE, D_RNN // LANES, ROW_TILE, LANES), F32)
    weights = (w_in, w_out, w_mix)
    return pl.pallas_call(
        _ffn_inproj_kernel,
        out_shape=[jax.ShapeDtypeStruct(x.shape, F32)] + [s_nat] * 3 + [s_seg] * 2,
        grid=(s // ROW_TILE,),
        in_specs=[row, _resident(mod9.shape), _resident(g.shape)]
                 + [pl.BlockSpec(memory_space=pl.ANY)] * len(weights),
        out_specs=[row] + [nat] * 3 + [seg] * 2,
        scratch_shapes=[pltpu.VMEM((ROW_TILE, D_MODEL), F32)]
                       + _weight_scratch(weights),
        compiler_params=pltpu.CompilerParams(
            dimension_semantics=("arbitrary",), vmem_limit_bytes=58 * MIB),
        name="ffn_inproj",
    )(x, mod9, g, w_in, w_out, w_mix)


def _attn_kernel(q_ref, k_ref, v_ref, o_ref,
                 kc1, vc1, kc4, vc4, kc16, vc16, q4, q16, g4_ref,
                 bias_ref, s_ref, p_ref, max_ref, acc_ref, den_ref, nat_ref):
    chunk = pl.program_id(1)
    nblk = ATTN_CHUNK // ATTN_BLOCK
    sub_rows = 4 * ATTN_BLOCK
    quarter = ATTN_BLOCK // 4
    lane = lax.broadcasted_iota(jnp.int32, (1, PAIR), 1)
    head_mask = [(lane < HEAD_DIM).astype(BF16), (lane >= HEAD_DIM).astype(BF16)]

    @pl.when((chunk == 0) & (pl.program_id(0) == 0))
    def _():
        for vc in (vc1, vc4, vc16):
            for h in range(2):
                vc[h, :, :, PAIR:] = jnp.broadcast_to(
                    head_mask[h], vc.shape[1:3] + (PAIR,))
        row = lax.broadcasted_iota(jnp.int32, (ATTN_BLOCK, 2 * ATTN_BLOCK), 0)
        col = lax.broadcasted_iota(jnp.int32, (ATTN_BLOCK, 2 * ATTN_BLOCK), 1)
        row4 = (row % quarter) * 4 + row // quarter
        for kind, r in enumerate((row, row4)):
            band = (col >= r) & (col <= r + N_BACK)
            bias_ref[2 * kind] = jnp.where(band, 0.0, -jnp.inf)
            bias_ref[2 * kind + 1] = jnp.where(band & (col >= ATTN_BLOCK), 0.0, -jnp.inf)

    prev = slice(0, ATTN_BLOCK)
    for kc, vc in ((kc1, vc1), (kc4, vc4), (kc16, vc16)):
        last = slice(kc.shape[1] - ATTN_BLOCK, kc.shape[1])

        @pl.when(chunk == 0)
        def _():
            kc[:, prev, :] = jnp.zeros((kc.shape[0], ATTN_BLOCK, PAIR), BF16)
            vc[:, :, prev, :PAIR] = jnp.zeros((2, vc.shape[1], ATTN_BLOCK, PAIR), BF16)

        @pl.when(chunk > 0)
        def _():
            kc[:, prev, :] = kc[:, last, :]
            vc[:, :, prev, :PAIR] = vc[:, :, last, :PAIR]

    def keep(what, g, k_dst, v_dst, q_dst):
        g = g.astype(BF16)
        if what == "q":
            q_dst(g)
        elif what == "k":
            k_dst(g)
        else:
            for h in range(2):
                v_dst(h, g * head_mask[h])

    for what, src in (("q", q_ref), ("k", k_ref), ("v", v_ref)):
        if what == "k":
            kc1[0, ATTN_BLOCK:, :] = src[...].astype(BF16)
        elif what == "v":
            nat = src[...].astype(BF16)
            for h in range(2):
                vc1[h, 0, ATTN_BLOCK:, :PAIR] = nat * head_mask[h]
        for i in range(ATTN_CHUNK // sub_rows):
            cur = slice((i + 1) * ATTN_BLOCK, (i + 2) * ATTN_BLOCK)
            for r4 in range(4):
                g = src[pl.ds(i * sub_rows + r4, ATTN_BLOCK, stride=4), :]
                g4_ref[r4, i * ATTN_BLOCK:(i + 1) * ATTN_BLOCK, :] = g

                def q_dst(b, i=i, r4=r4):
                    q4[i, r4] = b

                def k_dst(b, cur=cur, r4=r4):
                    kc4[r4, cur, :] = b

                def v_dst(h, b, cur=cur, r4=r4):
                    vc4[h, r4, cur, :PAIR] = b

                keep(what, g, k_dst, v_dst, q_dst)
        for r4 in range(4):
            for e in range(4):
                r = r4 + 4 * e
                g = g4_ref[r4, pl.ds(e, ATTN_BLOCK, stride=4), :]

                def q_dst(b, r=r):
                    q16[r] = b

                def k_dst(b, r=r):
                    kc16[r, ATTN_BLOCK:, :] = b

                def v_dst(h, b, r=r):
                    vc16[h, r, ATTN_BLOCK:, :PAIR] = b

                keep(what, g, k_dst, v_dst, q_dst)

    max_ref[...] = jnp.full(max_ref.shape, -jnp.inf, F32)
    acc_ref[...] = jnp.zeros(acc_ref.shape, F32)
    den_ref[...] = jnp.zeros(den_ref.shape, F32)

    def window(j):
        return pl.ds(j * ATTN_BLOCK, 2 * ATTN_BLOCK)

    def dilation1(j):
        i, jj = divmod(j, 4)
        rows = slice(jj * quarter, (jj + 1) * quarter)
        q = jnp.concatenate([q4[i, r4, rows, :] for r4 in range(4)], axis=0)
        pieces = [(pl.ds(i * sub_rows + r4 * ATTN_BLOCK + jj * quarter, quarter),
                   slice(r4 * quarter, (r4 + 1) * quarter)) for r4 in range(4)]
        return (q, lambda: kc1[0, window(j), :], lambda h: vc1[h, 0, window(j), :],
                True if j > 0 else chunk > 0, 1, pieces)

    def dilation4(t):
        i, r4 = divmod(t, 4)
        pieces = [(pl.ds(i * sub_rows + r4 * ATTN_BLOCK, ATTN_BLOCK),
                   slice(0, ATTN_BLOCK))]
        return (q4[i, r4], lambda: kc4[r4, window(i), :],
                lambda h: vc4[h, r4, window(i), :],
                True if i > 0 else chunk > 0, 0, pieces)

    def dilation16(r):
        e, r4 = divmod(r, 4)
        pieces = [(pl.ds(i * sub_rows + r4 * ATTN_BLOCK + e, quarter, stride=4),
                   slice(i * quarter, (i + 1) * quarter)) for i in range(4)]
        return (q16[r], lambda: kc16[r], lambda h: vc16[h, r],
                chunk > 0, 0, pieces)

    makers = [dilation4] * nblk + [dilation16] * nblk + [dilation1] * nblk
    n_tiles = len(makers)

    def scores(t):
        q, keys, _, has_prev, kind, pieces = makers[t](t % nblk)
        no_prev = 0 if has_prev is True else jnp.where(has_prev, 0, 1)
        bias = bias_ref[2 * kind + no_prev]
        kk = keys()
        for h in range(2):
            s = lax.dot_general(q * head_mask[h], kk, (((1,), (1,)), ((), ())),
                                preferred_element_type=F32) + bias
            s_ref[t, h] = s
            m = jnp.broadcast_to(jnp.max(s, axis=-1, keepdims=True),
                                 (ATTN_BLOCK, PAIR))
            for canon, rows in pieces:
                max_ref[h, canon, :] = jnp.maximum(max_ref[h, canon, :], m[rows])

    def softmax(t):
        pieces = makers[t](t % nblk)[5]
        for h in range(2):
            m = jnp.concatenate([max_ref[h, canon, :] for canon, _ in pieces], axis=0)
            p = jnp.exp2(s_ref[t, h] - jnp.concatenate([m, m], axis=1))
            p_ref[t % nblk, h] = p.astype(BF16)

    def values(t):
        _, _, vals, _, _, pieces = makers[t](t % nblk)
        pv = (jnp.dot(p_ref[t % nblk, 0], vals(0), preferred_element_type=F32)
              + jnp.dot(p_ref[t % nblk, 1], vals(1), preferred_element_type=F32))
        for canon, rows in pieces:
            acc_ref[canon, :] += pv[rows, :PAIR]
            den_ref[canon, :] += pv[rows, PAIR:]

    def pipeline(stages):
        n_groups = n_tiles // ATTN_GROUP
        for i in range(n_groups + len(stages) - 1):
            for lag in reversed(range(len(stages))):
                if 0 <= i - lag < n_groups:
                    for j in range(ATTN_GROUP):
                        stages[lag]((i - lag) * ATTN_GROUP + j)

    pipeline([scores])
    pipeline([softmax, values])

    for i in range(ATTN_CHUNK // sub_rows):
        for r4 in range(4):
            canon = pl.ds(i * sub_rows + r4 * ATTN_BLOCK, ATTN_BLOCK)
            nat_ref[pl.ds(i * sub_rows + r4, ATTN_BLOCK, stride=4), :] = (
                acc_ref[canon, :] / den_ref[canon, :])
    o_ref[...] = nat_ref[...].astype(BF16)


def _attn_call(q, k, v):
    s = q.shape[0]
    n_pairs = D_ATTN // PAIR
    blocks_per_chunk = ATTN_CHUNK // ATTN_BLOCK
    sub_per_chunk = ATTN_CHUNK // (4 * ATTN_BLOCK)
    nat = pl.BlockSpec((ATTN_CHUNK, PAIR), lambda p, c: (c, p))
    kcat = lambda classes, blocks: pltpu.VMEM(
        (classes, (blocks + 1) * ATTN_BLOCK, PAIR), BF16)
    vcat = lambda classes, blocks: pltpu.VMEM(
        (2, classes, (blocks + 1) * ATTN_BLOCK, 2 * PAIR), BF16)
    rows = lambda: pltpu.VMEM((ATTN_CHUNK, PAIR), F32)
    return pl.pallas_call(
        _attn_kernel,
        out_shape=jax.ShapeDtypeStruct((s, D_ATTN), BF16),
        grid=(n_pairs, s // ATTN_CHUNK),
        in_specs=[nat] * 3,
        out_specs=nat,
        scratch_shapes=[
            kcat(1, 16), vcat(1, 16), kcat(4, 4), vcat(4, 4), kcat(16, 1), vcat(16, 1),
            pltpu.VMEM((sub_per_chunk, 4, ATTN_BLOCK, PAIR), BF16),
            pltpu.VMEM((16, ATTN_BLOCK, PAIR), BF16),
            pltpu.VMEM((4, ATTN_CHUNK // 4, PAIR), F32),
            pltpu.VMEM((4, ATTN_BLOCK, 2 * ATTN_BLOCK), F32),
            pltpu.VMEM((3 * blocks_per_chunk, 2, ATTN_BLOCK, 2 * ATTN_BLOCK), F32),
            pltpu.VMEM((blocks_per_chunk, 2, ATTN_BLOCK, 2 * ATTN_BLOCK), BF16),
            pltpu.VMEM((2, ATTN_CHUNK, PAIR), F32),
            rows(), rows(), rows()],
        compiler_params=pltpu.CompilerParams(
            dimension_semantics=("arbitrary", "arbitrary"),
            vmem_limit_bytes=56 * MIB),
        name="attn",
    )(q, k, v)


def _rglru_phases(xr_ref, prev_rows, gr_ref, params, rec_ref, scratch):
    cw_ref, cb_ref, wx_ref, wa_ref, bx_ref, ba_ref, lam_ref = params
    xcat_ref, a_ref, b_ref, h_ref, nat_ref, carry_ref = scratch
    half = D_RNN // 2
    n_slabs = D_RNN // LANES
    halo = SUBLANES * (CONV_WIDTH - 1)
    part = ROW_TILE // LRU_PARTS

    def wide(ref, rows):
        return jnp.concatenate([ref[0, t, rows, :] for t in range(n_slabs)], axis=1)

    sub = lax.broadcasted_iota(jnp.int32, (SUBLANES, D_RNN), 0)
    for d in range(1, CONV_WIDTH):
        cur = wide(xr_ref, pl.ds(SUBLANES * (SEG_LEN - d), 8))
        xcat_ref[pl.ds(halo - SUBLANES * d, SUBLANES), :] = jnp.where(
            sub == 0, pltpu.roll(prev_rows(d), 1, 0), pltpu.roll(cur, 1, 0))
    xcat_ref[halo:, :] = wide(xr_ref, slice(None))
    yield

    neg_lam = -lam_ref[...]
    softplus = jnp.maximum(neg_lam, 0.0) + jnp.log1p(jnp.exp(-jnp.abs(neg_lam)))
    seg_end = jnp.zeros((SUBLANES, D_RNN), F32)
    seg_prod = jnp.ones((SUBLANES, D_RNN), F32)
    for p in range(LRU_PARTS):
        rows = pl.ds(p * part, part)
        u = cb_ref[...]
        for k in range(CONV_WIDTH):
            d = CONV_WIDTH - 1 - k
            u = u + cw_ref[k:k + 1, :] * xcat_ref[pl.ds(halo - SUBLANES * d + p * part, part), :]
        ub = u.astype(BF16)

        def gate(w_ref, b_ref):
            lo = jnp.dot(ub[:, :half], w_ref[0], preferred_element_type=F32)
            hi = jnp.dot(ub[:, half:], w_ref[1], preferred_element_type=F32)
            z = jnp.concatenate([lo, hi], axis=1) + b_ref[...]
            return 0.5 * jnp.tanh(0.5 * z) + 0.5

        gate_x = gate(wx_ref, bx_ref)
        gate_a = gate(wa_ref, ba_ref)
        log_a = -LRU_C * gate_a * softplus
        a = jnp.exp(log_a)
        one_minus_a2 = -jnp.tanh(log_a) * (a * a + 1.0)
        root = jnp.where(one_minus_a2 > 0.0,
                         one_minus_a2 * lax.rsqrt(one_minus_a2), 0.0)
        a_ref[rows, :] = a
        b_ref[rows, :] = u * gate_x * root
        yield

        for j in range(p * part // SUBLANES, (p + 1) * part // SUBLANES):
            step_rows = pl.ds(j * SUBLANES, SUBLANES)
            a_j = a_ref[step_rows, :]
            seg_end = a_j * seg_end + b_ref[step_rows, :]
            seg_prod = a_j * seg_prod
            h_ref[step_rows, :] = seg_end
            a_ref[step_rows, :] = seg_prod
        yield

    entering = [carry_ref[0:1, :]]
    for s in range(SUBLANES):
        entering.append(seg_end[s:s + 1, :] + seg_prod[s:s + 1, :] * entering[s])
    carry_ref[0:1, :] = entering[SUBLANES]
    state_in = jnp.tile(jnp.concatenate(entering[:SUBLANES], axis=0),
                        (part // SUBLANES, 1))

    for p in range(LRU_PARTS):
        rows = pl.ds(p * part, part)
        g = wide(gr_ref, rows)
        g_half = 0.5 * g
        inner = g * (math.sqrt(2.0 / math.pi)
                     + (0.044715 * math.sqrt(2.0 / math.pi)) * (g * g))
        gelu = g_half + g_half * jnp.tanh(inner)
        out = (h_ref[rows, :] + a_ref[rows, :] * state_in) * gelu
        for t in range(n_slabs):
            nat_ref[t, rows, :] = out[:, t * LANES:(t + 1) * LANES]
        yield

    for t in range(n_slabs):
        for s in range(SUBLANES):
            rec_ref[s * SEG_LEN:(s + 1) * SEG_LEN, t * LANES:(t + 1) * LANES] = (
                nat_ref[t, pl.ds(s, SEG_LEN, stride=SUBLANES), :].astype(BF16))
        yield


def _mixer_ffn_kernel(x_ref, attn_ref, xr0_ref, gr0_ref, xrn_ref, xpn_ref, grn_ref,
                      mod_ref, g_ref, *refs):
    lru_params = refs[:7]
    o_ref, acc_ref, rec_ref = refs[10:13]
    lru_scratch = refs[13:19]
    wmix_ref, win_ref, wout_ref = _load_weights_once(refs[7:10], refs[19:])
    step = pl.program_id(0)
    n_slabs = D_RNN // LANES

    @pl.when(step == 0)
    def _():
        lru_scratch[-1][...] = jnp.zeros_like(lru_scratch[-1])
        for _ in _rglru_phases(xr0_ref, lambda d: jnp.zeros((SUBLANES, D_RNN), F32), gr0_ref,
                               lru_params, rec_ref.at[0], lru_scratch):
            pass

    def prev_rows(d):
        return jnp.concatenate(
            [xpn_ref[0, t, pl.ds(SUBLANES * (CONV_WIDTH - d), SUBLANES), :] for t in range(n_slabs)],
            axis=1)

    cur = step % 2
    next_rec = _rglru_phases(xrn_ref, prev_rows, grn_ref, lru_params,
                             rec_ref.at[1 - cur], lru_scratch)
    y = jnp.dot(attn_ref[...], wmix_ref[:D_ATTN, :], preferred_element_type=F32)
    y = y + jnp.dot(rec_ref[cur], wmix_ref[D_ATTN:, :], preferred_element_type=F32)
    x2 = _post_residual(x_ref[...], y, mod_ref, g_ref, 1, 1.0)
    o_ref[...] = _ffn(2, x2, mod_ref, g_ref, win_ref, wout_ref, acc_ref,
                      side_work=next_rec)


def _mixer_ffn_call(x, attn, xr, gr, mod9, g, lru_params, w_mix, w_in, w_out):
    n_tiles, n_slabs = xr.shape[0], xr.shape[1]
    row = pl.BlockSpec((ROW_TILE, D_MODEL), lambda i: (i, 0))
    half = pl.BlockSpec((ROW_TILE, D_ATTN), lambda i: (i, 0))
    nxt = lambda i: jnp.minimum(i + 1, n_tiles - 1)
    first_tile = pl.BlockSpec((1, n_slabs, ROW_TILE, LANES), lambda i: (0, 0, 0, 0),
                              pipeline_mode=pl.Buffered(1))
    next_tile = pl.BlockSpec((1, n_slabs, ROW_TILE, LANES), lambda i: (nxt(i), 0, 0, 0))
    last_rows = SUBLANES * CONV_WIDTH
    before_next = pl.BlockSpec(
        (1, n_slabs, last_rows, LANES),
        lambda i: (jnp.maximum(nxt(i) - 1, 0), 0, ROW_TILE // last_rows - 1, 0))
    weights = (w_mix, w_in, w_out)
    return pl.pallas_call(
        _mixer_ffn_kernel,
        out_shape=jax.ShapeDtypeStruct(x.shape, F32),
        grid=(n_tiles,),
        in_specs=[row, half, first_tile, first_tile, next_tile, before_next, next_tile,
                  _resident(mod9.shape), _resident(g.shape)]
                 + [_resident(p.shape) for p in lru_params]
                 + [pl.BlockSpec(memory_space=pl.ANY)] * len(weights),
        out_specs=row,
        scratch_shapes=[pltpu.VMEM((ROW_TILE, D_MODEL), F32),
                        pltpu.VMEM((2, ROW_TILE, D_RNN), BF16),
                        pltpu.VMEM((ROW_TILE + SUBLANES * (CONV_WIDTH - 1), D_RNN), F32),
                        pltpu.VMEM((ROW_TILE, D_RNN), F32),
                        pltpu.VMEM((ROW_TILE, D_RNN), F32),
                        pltpu.VMEM((ROW_TILE, D_RNN), F32),
                        pltpu.VMEM((n_slabs, ROW_TILE, LANES), F32),
                        pltpu.VMEM((SUBLANES, D_RNN), F32)]
                       + _weight_scratch(weights),
        compiler_params=pltpu.CompilerParams(
            dimension_semantics=("arbitrary",), vmem_limit_bytes=56 * MIB),
        name="mixer_ffn",
    )(x, attn, xr, gr, xr, xr, gr, mod9, g, *lru_params, w_mix, w_in, w_out)


def _blockdiag_halves(w):
    per_half = (D_RNN // 2) // LRU_BLOCK
    eye = jnp.eye(per_half, dtype=w.dtype)
    wh = w.reshape(2, per_half, LRU_BLOCK, LRU_BLOCK)
    dense = jnp.einsum('hgij,gk->hgikj', wh, eye)
    return dense.reshape(2, D_RNN // 2, D_RNN // 2).astype(BF16)


def kernel(x, c, w_ada, b_ada, norm_gain, ffn1_w_in, ffn1_w_out, mix_w_in, conv_w, conv_b,
           lru_w_x, lru_b_x, lru_w_a, lru_b_a, lru_lambda, mix_w_out, ffn2_w_in, ffn2_w_out):
    batch, seq, d = x.shape
    assert (batch, seq, d) == (1, SEQ, D_MODEL) and w_ada.shape[0] == 1
    xs = x.reshape(seq, d)
    g = norm_gain[0]

    mod = _mod_call(c.reshape(d, 1), w_ada[0], b_ada)
    mod9 = mod.reshape(9, d)

    x1, q, k, v, xr, gr = _ffn_inproj_call(
        xs, mod9, g, ffn1_w_in[0], ffn1_w_out[0], mix_w_in[0])
    attn = _attn_call(q, k, v)
    lru_params = (conv_w[0], conv_b, _blockdiag_halves(lru_w_x[0]),
                  _blockdiag_halves(lru_w_a[0]), lru_b_x, lru_b_a, lru_lambda)
    x3 = _mixer_ffn_call(x1, attn, xr, gr, mod9, g, lru_params, mix_w_out[0],
                         ffn2_w_in[0], ffn2_w_out[0])
    return x3.reshape(batch, seq, d)
```

```python
import math

import jax
import jax.numpy as jnp
from jax import lax
from jax.experimental import pallas as pl
from jax.experimental.pallas import tpu as pltpu

F32 = jnp.float32
BF16 = jnp.bfloat16

D_MODEL = 1024
SEQ = 16384
HEAD_DIM = 64
D_ATTN = 512
D_RNN = 512
LRU_BLOCK = 64
N_BACK = 128
ATTN_BLOCK = 128
CONV_WIDTH = 4
LRU_C = 8.0
D_FF = 2816
FFN_RES_WEIGHT = 0.5
RMS_EPS = 1e-6

MIB = 1024 * 1024
SUBLANES = 8
ROW_TILE = 512
SEG_LEN = ROW_TILE // SUBLANES
LRU_PARTS = 4
PRE_PARTS = 16
FF_CHUNK = 256
ATTN_CHUNK = 2048
LANES = 128
STAGE_ROWS, STAGE_COLS = 256, 512
STAGE_SLOTS = 12
PAIR = 2 * HEAD_DIM
MOD_TILE = 2304
ATTN_GROUP = 4


def _resident(shape):
    return pl.BlockSpec(shape, lambda *_: (0,) * len(shape),
                        pipeline_mode=pl.Buffered(1))


def _weight_scratch(weights):
    for w in weights:
        assert w.shape[0] % STAGE_ROWS == 0 and w.shape[1] % STAGE_COLS == 0, w.shape
    return ([pltpu.VMEM(w.shape, BF16) for w in weights]
            + [pltpu.VMEM((STAGE_SLOTS, STAGE_ROWS, STAGE_COLS), F32),
               pltpu.SemaphoreType.DMA((STAGE_SLOTS,))])


def _load_weights_once(hbm_refs, scratch_refs):
    *dst_refs, stage_ref, sem = scratch_refs
    blocks = [(hbm, dst, r, c)
              for hbm, dst in zip(hbm_refs, dst_refs)
              for r in range(0, dst.shape[0], STAGE_ROWS)
              for c in range(0, dst.shape[1], STAGE_COLS)]

    def copy(i):
        hbm, _, r, c = blocks[i]
        slot = i % STAGE_SLOTS
        return pltpu.make_async_copy(
            hbm.at[pl.ds(r, STAGE_ROWS), pl.ds(c, STAGE_COLS)],
            stage_ref.at[slot], sem.at[slot])

    @pl.when(pl.program_id(0) == 0)
    def _():
        for i in range(min(STAGE_SLOTS, len(blocks))):
            copy(i).start()
        for i, (_, dst, r, c) in enumerate(blocks):
            copy(i).wait()
            dst[r:r + STAGE_ROWS, c:c + STAGE_COLS] = (
                stage_ref[i % STAGE_SLOTS].astype(BF16))
            if i + STAGE_SLOTS < len(blocks):
                copy(i + STAGE_SLOTS).start()

    return dst_refs


def _rms(v):
    return v * lax.rsqrt(jnp.mean(v * v, axis=-1, keepdims=True) + RMS_EPS)


def _pre(x, mod_ref, g_ref, sub):
    shift = mod_ref[3 * sub:3 * sub + 1, :]
    scale = mod_ref[3 * sub + 1:3 * sub + 2, :]
    gain = g_ref[2 * sub:2 * sub + 1, :] * (1.0 + scale)
    return _rms(x) * gain + shift


def _post_residual(x, y, mod_ref, g_ref, sub, weight):
    gate = mod_ref[3 * sub + 2:3 * sub + 3, :]
    gain = weight * (1.0 + gate) * g_ref[2 * sub + 1:2 * sub + 2, :]
    return x + _rms(y) * gain


def _mod_kernel(c_ref, w_ref, b_ref, o_ref):
    c = c_ref[...]
    s = c * jax.nn.sigmoid(c)
    o_ref[...] = jnp.sum(s * w_ref[...], axis=0, keepdims=True) + b_ref[...]


def _mod_call(c_col, w_ada, b_ada):
    n = w_ada.shape[1]
    return pl.pallas_call(
        _mod_kernel,
        out_shape=jax.ShapeDtypeStruct((1, n), F32),
        grid=(n // MOD_TILE,),
        in_specs=[
            pl.BlockSpec((D_MODEL, 1), lambda j: (0, 0)),
            pl.BlockSpec((D_MODEL, MOD_TILE), lambda j: (0, j)),
            pl.BlockSpec((1, MOD_TILE), lambda j: (0, j)),
        ],
        out_specs=pl.BlockSpec((1, MOD_TILE), lambda j: (0, j)),
        compiler_params=pltpu.CompilerParams(
            dimension_semantics=("arbitrary",), vmem_limit_bytes=24 * MIB),
        name="mod",
    )(c_col, w_ada, b_ada)


def _ffn(sub, x, mod_ref, g_ref, win_ref, wout_ref, acc_ref, side_work=(), h=None):
    side_work = iter(side_work)
    if h is None:
        h = _pre(x, mod_ref, g_ref, sub).astype(BF16)
    for lo in range(0, D_FF, FF_CHUNK):
        hi = min(lo + FF_CHUNK, D_FF)
        hg = jnp.dot(h, win_ref[:, lo:hi], preferred_element_type=F32)
        hu = jnp.dot(h, win_ref[:, D_FF + lo:D_FF + hi], preferred_element_type=F32)
        a = (hg * jax.nn.sigmoid(hg) * hu).astype(BF16)
        part = jnp.dot(a, wout_ref[lo:hi, :], preferred_element_type=F32)
        if lo == 0:
            acc_ref[...] = part
        else:
            acc_ref[...] += part
        for _ in range(2):
            next(side_work, None)
    for _ in side_work:
        pass
    return _post_residual(x, acc_ref[...], mod_ref, g_ref, sub, FFN_RES_WEIGHT)


def _ffn_inproj_kernel(x_ref, xn_ref, mod_ref, g_ref, *refs):
    x1_ref, q_ref, k_ref, v_ref, xr_ref, gr_ref, acc_ref, h_ref = refs[3:11]
    win_ref, wout_ref, wmix_ref = _load_weights_once(refs[:3], refs[11:])
    step = pl.program_id(0)
    cur = step % 2

    @pl.when(step == 0)
    def _():
        h_ref[0] = _pre(x_ref[...], mod_ref, g_ref, 0).astype(BF16)

    def next_pre_norm():
        rows_per_phase = ROW_TILE // PRE_PARTS
        for p in range(PRE_PARTS):
            rows = pl.ds(p * rows_per_phase, rows_per_phase)
            h_ref[1 - cur, rows, :] = _pre(xn_ref[rows, :], mod_ref, g_ref, 0).astype(BF16)
            yield

    x1 = _ffn(0, x_ref[...], mod_ref, g_ref, win_ref, wout_ref, acc_ref,
              side_work=next_pre_norm(), h=h_ref[cur])
    x1_ref[...] = x1
    h = _pre(x1, mod_ref, g_ref, 1).astype(BF16)

    def proj(j, width):
        return jnp.dot(h, wmix_ref[:, j:j + width], preferred_element_type=F32)

    q_ref[...] = proj(0, D_ATTN) * (HEAD_DIM ** -0.5 * math.log2(math.e))
    k_ref[...] = proj(D_ATTN, D_ATTN)
    v_ref[...] = proj(2 * D_ATTN, D_ATTN)

    for n, dst in enumerate((xr_ref, gr_ref)):
        z = proj(3 * D_ATTN + n * D_RNN, D_RNN)
        for t in range(D_RNN // LANES):
            for s in range(SUBLANES):
                dst[0, t, pl.ds(s, SEG_LEN, stride=SUBLANES), :] = z[
                    s * SEG_LEN:(s + 1) * SEG_LEN, t * LANES:(t + 1) * LANES]


def _ffn_inproj_call(x, mod9, g, w_in, w_out, w_mix):
    s = x.shape[0]
    row = pl.BlockSpec((ROW_TILE, D_MODEL), lambda i: (i, 0))
    next_row = pl.BlockSpec((ROW_TILE, D_MODEL),
                            lambda i: (jnp.minimum(i + 1, s // ROW_TILE - 1), 0))
    nat = pl.BlockSpec((ROW_TILE, D_ATTN), lambda i: (i, 0))
    s_nat = jax.ShapeDtypeStruct((s, D_ATTN), F32)
    seg = pl.BlockSpec((1, D_RNN // LANES, ROW_TILE, LANES), lambda i: (i, 0, 0, 0))
    s_seg = jax.ShapeDtypeStruct((s // ROW_TILE, D_RNN // LANES, ROW_TILE, LANES), F32)
    weights = (w_in, w_out, w_mix)
    return pl.pallas_call(
        _ffn_inproj_kernel,
        out_shape=[jax.ShapeDtypeStruct(x.shape, F32)] + [s_nat] * 3 + [s_seg] * 2,
        grid=(s // ROW_TILE,),
        in_specs=[row, next_row, _resident(mod9.shape), _resident(g.shape)]
                 + [pl.BlockSpec(memory_space=pl.ANY)] * len(weights),
        out_specs=[row] + [nat] * 3 + [seg] * 2,
        scratch_shapes=[pltpu.VMEM((ROW_TILE, D_MODEL), F32),
                        pltpu.VMEM((2, ROW_TILE, D_MODEL), BF16)]
                       + _weight_scratch(weights),
        compiler_params=pltpu.CompilerParams(
            dimension_semantics=("arbitrary",), vmem_limit_bytes=58 * MIB),
        name="ffn_inproj",
    )(x, x, mod9, g, w_in, w_out, w_mix)


def _attn_kernel(q_ref, k_ref, v_ref, o_ref,
                 kc1, vc1, kc4, vc4, kc16, vc16, q4, q16, g4_ref,
                 bias_ref, s_ref, p_ref, max_ref, acc_ref, den_ref, nat_ref):
    chunk = pl.program_id(1)
    nblk = ATTN_CHUNK // ATTN_BLOCK
    sub_rows = 4 * ATTN_BLOCK
    quarter = ATTN_BLOCK // 4
    lane = lax.broadcasted_iota(jnp.int32, (1, PAIR), 1)
    head_mask = [(lane < HEAD_DIM).astype(BF16), (lane >= HEAD_DIM).astype(BF16)]

    @pl.when((chunk == 0) & (pl.program_id(0) == 0))
    def _():
        for vc in (vc1, vc4, vc16):
            for h in range(2):
                vc[h, :, :, PAIR:] = jnp.broadcast_to(
                    head_mask[h], vc.shape[1:3] + (PAIR,))
        row = lax.broadcasted_iota(jnp.int32, (ATTN_BLOCK, 2 * ATTN_BLOCK), 0)
        col = lax.broadcasted_iota(jnp.int32, (ATTN_BLOCK, 2 * ATTN_BLOCK), 1)
        row4 = (row % quarter) * 4 + row // quarter
        for kind, r in enumerate((row, row4)):
            band = (col >= r) & (col <= r + N_BACK)
            bias_ref[2 * kind] = jnp.where(band, 0.0, -jnp.inf)
            bias_ref[2 * kind + 1] = jnp.where(band & (col >= ATTN_BLOCK), 0.0, -jnp.inf)

    prev = slice(0, ATTN_BLOCK)
    for kc, vc in ((kc1, vc1), (kc4, vc4), (kc16, vc16)):
        last = slice(kc.shape[1] - ATTN_BLOCK, kc.shape[1])

        @pl.when(chunk == 0)
        def _():
            kc[:, prev, :] = jnp.zeros((kc.shape[0], ATTN_BLOCK, PAIR), BF16)
            vc[:, :, prev, :PAIR] = jnp.zeros((2, vc.shape[1], ATTN_BLOCK, PAIR), BF16)

        @pl.when(chunk > 0)
        def _():
            kc[:, prev, :] = kc[:, last, :]
            vc[:, :, prev, :PAIR] = vc[:, :, last, :PAIR]

    def keep(what, g, k_dst, v_dst, q_dst):
        g = g.astype(BF16)
        if what == "q":
            q_dst(g)
        elif what == "k":
            k_dst(g)
        else:
            for h in range(2):
                v_dst(h, g * head_mask[h])

    for what, src in (("q", q_ref), ("k", k_ref), ("v", v_ref)):
        if what == "k":
            kc1[0, ATTN_BLOCK:, :] = src[...].astype(BF16)
        elif what == "v":
            nat = src[...].astype(BF16)
            for h in range(2):
                vc1[h, 0, ATTN_BLOCK:, :PAIR] = nat * head_mask[h]
        for i in range(ATTN_CHUNK // sub_rows):
            cur = slice((i + 1) * ATTN_BLOCK, (i + 2) * ATTN_BLOCK)
            for r4 in range(4):
                g = src[pl.ds(i * sub_rows + r4, ATTN_BLOCK, stride=4), :]
                g4_ref[r4, i * ATTN_BLOCK:(i + 1) * ATTN_BLOCK, :] = g

                def q_dst(b, i=i, r4=r4):
                    q4[i, r4] = b

                def k_dst(b, cur=cur, r4=r4):
                    kc4[r4, cur, :] = b

                def v_dst(h, b, cur=cur, r4=r4):
                    vc4[h, r4, cur, :PAIR] = b

                keep(what, g, k_dst, v_dst, q_dst)
        for r4 in range(4):
            for e in range(4):
                r = r4 + 4 * e
                g = g4_ref[r4, pl.ds(e, ATTN_BLOCK, stride=4), :]

                def q_dst(b, r=r):
                    q16[r] = b

                def k_dst(b, r=r):
                    kc16[r, ATTN_BLOCK:, :] = b

                def v_dst(h, b, r=r):
                    vc16[h, r, ATTN_BLOCK:, :PAIR] = b

                keep(what, g, k_dst, v_dst, q_dst)

    max_ref[...] = jnp.full(max_ref.shape, -jnp.inf, F32)
    acc_ref[...] = jnp.zeros(acc_ref.shape, F32)
    den_ref[...] = jnp.zeros(den_ref.shape, F32)

    def window(j):
        return pl.ds(j * ATTN_BLOCK, 2 * ATTN_BLOCK)

    def dilation1(j):
        i, jj = divmod(j, 4)
        rows = slice(jj * quarter, (jj + 1) * quarter)
        q = jnp.concatenate([q4[i, r4, rows, :] for r4 in range(4)], axis=0)
        pieces = [(pl.ds(i * sub_rows + r4 * ATTN_BLOCK + jj * quarter, quarter),
                   slice(r4 * quarter, (r4 + 1) * quarter)) for r4 in range(4)]
        return (q, lambda: kc1[0, window(j), :], lambda h: vc1[h, 0, window(j), :],
                True if j > 0 else chunk > 0, 1, pieces)

    def dilation4(t):
        i, r4 = divmod(t, 4)
        pieces = [(pl.ds(i * sub_rows + r4 * ATTN_BLOCK, ATTN_BLOCK),
                   slice(0, ATTN_BLOCK))]
        return (q4[i, r4], lambda: kc4[r4, window(i), :],
                lambda h: vc4[h, r4, window(i), :],
                True if i > 0 else chunk > 0, 0, pieces)

    def dilation16(r):
        e, r4 = divmod(r, 4)
        pieces = [(pl.ds(i * sub_rows + r4 * ATTN_BLOCK + e, quarter, stride=4),
                   slice(i * quarter, (i + 1) * quarter)) for i in range(4)]
        return (q16[r], lambda: kc16[r], lambda h: vc16[h, r],
                chunk > 0, 0, pieces)

    makers = [dilation4] * nblk + [dilation16] * nblk + [dilation1] * nblk
    n_tiles = len(makers)

    def scores(t):
        q, keys, _, has_prev, kind, pieces = makers[t](t % nblk)
        no_prev = 0 if has_prev is True else jnp.where(has_prev, 0, 1)
        bias = bias_ref[2 * kind + no_prev]
        kk = keys()
        for h in range(2):
            s = lax.dot_general(q * head_mask[h], kk, (((1,), (1,)), ((), ())),
                                preferred_element_type=F32) + bias
            s_ref[t, h] = s
            m = jnp.broadcast_to(jnp.max(s, axis=-1, keepdims=True),
                                 (ATTN_BLOCK, PAIR))
            for canon, rows in pieces:
                max_ref[h, canon, :] = jnp.maximum(max_ref[h, canon, :], m[rows])

    def softmax(t):
        pieces = makers[t](t % nblk)[5]
        for h in range(2):
            m = jnp.concatenate([max_ref[h, canon, :] for canon, _ in pieces], axis=0)
            p = jnp.exp2(s_ref[t, h] - jnp.concatenate([m, m], axis=1))
            p_ref[t % nblk, h] = p.astype(BF16)

    def values(t):
        _, _, vals, _, _, pieces = makers[t](t % nblk)
        pv = (jnp.dot(p_ref[t % nblk, 0], vals(0), preferred_element_type=F32)
              + jnp.dot(p_ref[t % nblk, 1], vals(1), preferred_element_type=F32))
        for canon, rows in pieces:
            acc_ref[canon, :] += pv[rows, :PAIR]
            den_ref[canon, :] += pv[rows, PAIR:]

    def pipeline(stages):
        n_groups = n_tiles // ATTN_GROUP
        for i in range(n_groups + len(stages) - 1):
            for lag in reversed(range(len(stages))):
                if 0 <= i - lag < n_groups:
                    for j in range(ATTN_GROUP):
                        stages[lag]((i - lag) * ATTN_GROUP + j)

    pipeline([scores])
    pipeline([softmax, values])

    for i in range(ATTN_CHUNK // sub_rows):
        for r4 in range(4):
            canon = pl.ds(i * sub_rows + r4 * ATTN_BLOCK, ATTN_BLOCK)
            nat_ref[pl.ds(i * sub_rows + r4, ATTN_BLOCK, stride=4), :] = (
                acc_ref[canon, :] / den_ref[canon, :])
    o_ref[...] = nat_ref[...].astype(BF16)


def _attn_call(q, k, v):
    s = q.shape[0]
    n_pairs = D_ATTN // PAIR
    blocks_per_chunk = ATTN_CHUNK // ATTN_BLOCK
    sub_per_chunk = ATTN_CHUNK // (4 * ATTN_BLOCK)
    nat = pl.BlockSpec((ATTN_CHUNK, PAIR), lambda p, c: (c, p))
    kcat = lambda classes, blocks: pltpu.VMEM(
        (classes, (blocks + 1) * ATTN_BLOCK, PAIR), BF16)
    vcat = lambda classes, blocks: pltpu.VMEM(
        (2, classes, (blocks + 1) * ATTN_BLOCK, 2 * PAIR), BF16)
    rows = lambda: pltpu.VMEM((ATTN_CHUNK, PAIR), F32)
    return pl.pallas_call(
        _attn_kernel,
        out_shape=jax.ShapeDtypeStruct((s, D_ATTN), BF16),
        grid=(n_pairs, s // ATTN_CHUNK),
        in_specs=[nat] * 3,
        out_specs=nat,
        scratch_shapes=[
            kcat(1, 16), vcat(1, 16), kcat(4, 4), vcat(4, 4), kcat(16, 1), vcat(16, 1),
            pltpu.VMEM((sub_per_chunk, 4, ATTN_BLOCK, PAIR), BF16),
            pltpu.VMEM((16, ATTN_BLOCK, PAIR), BF16),
            pltpu.VMEM((4, ATTN_CHUNK // 4, PAIR), F32),
            pltpu.VMEM((4, ATTN_BLOCK, 2 * ATTN_BLOCK), F32),
            pltpu.VMEM((3 * blocks_per_chunk, 2, ATTN_BLOCK, 2 * ATTN_BLOCK), F32),
            pltpu.VMEM((blocks_per_chunk, 2, ATTN_BLOCK, 2 * ATTN_BLOCK), BF16),
            pltpu.VMEM((2, ATTN_CHUNK, PAIR), F32),
            rows(), rows(), rows()],
        compiler_params=pltpu.CompilerParams(
            dimension_semantics=("arbitrary", "arbitrary"),
            vmem_limit_bytes=56 * MIB),
        name="attn",
    )(q, k, v)


def _rglru_phases(xr_ref, prev_rows, gr_ref, params, rec_ref, scratch):
    cw_ref, cb_ref, wx_ref, wa_ref, bx_ref, ba_ref, lam_ref = params
    xcat_ref, a_ref, b_ref, h_ref, nat_ref, carry_ref = scratch
    half = D_RNN // 2
    n_slabs = D_RNN // LANES
    halo = SUBLANES * (CONV_WIDTH - 1)
    part = ROW_TILE // LRU_PARTS

    def wide(ref, rows):
        return jnp.concatenate([ref[0, t, rows, :] for t in range(n_slabs)], axis=1)

    sub = lax.broadcasted_iota(jnp.int32, (SUBLANES, D_RNN), 0)
    for d in range(1, CONV_WIDTH):
        cur = wide(xr_ref, pl.ds(SUBLANES * (SEG_LEN - d), 8))
        xcat_ref[pl.ds(halo - SUBLANES * d, SUBLANES), :] = jnp.where(
            sub == 0, pltpu.roll(prev_rows(d), 1, 0), pltpu.roll(cur, 1, 0))
    xcat_ref[halo:, :] = wide(xr_ref, slice(None))
    yield

    neg_lam = -lam_ref[...]
    softplus = jnp.maximum(neg_lam, 0.0) + jnp.log1p(jnp.exp(-jnp.abs(neg_lam)))
    seg_end = jnp.zeros((SUBLANES, D_RNN), F32)
    seg_prod = jnp.ones((SUBLANES, D_RNN), F32)
    for p in range(LRU_PARTS):
        rows = pl.ds(p * part, part)
        u = cb_ref[...]
        for k in range(CONV_WIDTH):
            d = CONV_WIDTH - 1 - k
            u = u + cw_ref[k:k + 1, :] * xcat_ref[pl.ds(halo - SUBLANES * d + p * part, part), :]
        ub = u.astype(BF16)

        def gate(w_ref, b_ref):
            lo = jnp.dot(ub[:, :half], w_ref[0], preferred_element_type=F32)
            hi = jnp.dot(ub[:, half:], w_ref[1], preferred_element_type=F32)
            z = jnp.concatenate([lo, hi], axis=1) + b_ref[...]
            return 0.5 * jnp.tanh(0.5 * z) + 0.5

        gate_x = gate(wx_ref, bx_ref)
        gate_a = gate(wa_ref, ba_ref)
        log_a = -LRU_C * gate_a * softplus
        a = jnp.exp(log_a)
        one_minus_a2 = -jnp.tanh(log_a) * (a * a + 1.0)
        root = jnp.where(one_minus_a2 > 0.0,
                         one_minus_a2 * lax.rsqrt(one_minus_a2), 0.0)
        a_ref[rows, :] = a
        b_ref[rows, :] = u * gate_x * root
        yield

        for j in range(p * part // SUBLANES, (p + 1) * part // SUBLANES):
            step_rows = pl.ds(j * SUBLANES, SUBLANES)
            a_j = a_ref[step_rows, :]
            seg_end = a_j * seg_end + b_ref[step_rows, :]
            seg_prod = a_j * seg_prod
            h_ref[step_rows, :] = seg_end
            a_ref[step_rows, :] = seg_prod
        yield

    entering = [carry_ref[0:1, :]]
    for s in range(SUBLANES):
        entering.append(seg_end[s:s + 1, :] + seg_prod[s:s + 1, :] * entering[s])
    carry_ref[0:1, :] = entering[SUBLANES]
    state_in = jnp.tile(jnp.concatenate(entering[:SUBLANES], axis=0),
                        (part // SUBLANES, 1))

    for p in range(LRU_PARTS):
        rows = pl.ds(p * part, part)
        g = wide(gr_ref, rows)
        g_half = 0.5 * g
        inner = g * (math.sqrt(2.0 / math.pi)
                     + (0.044715 * math.sqrt(2.0 / math.pi)) * (g * g))
        gelu = g_half + g_half * jnp.tanh(inner)
        out = (h_ref[rows, :] + a_ref[rows, :] * state_in) * gelu
        for t in range(n_slabs):
            nat_ref[t, rows, :] = out[:, t * LANES:(t + 1) * LANES]
        yield

    for t in range(n_slabs):
        for s in range(SUBLANES):
            rec_ref[s * SEG_LEN:(s + 1) * SEG_LEN, t * LANES:(t + 1) * LANES] = (
                nat_ref[t, pl.ds(s, SEG_LEN, stride=SUBLANES), :].astype(BF16))
        yield


def _mixer_ffn_kernel(x_ref, attn_ref, xr0_ref, gr0_ref, xrn_ref, xpn_ref, grn_ref,
                      mod_ref, g_ref, *refs):
    lru_params = refs[:7]
    o_ref, acc_ref, rec_ref = refs[10:13]
    lru_scratch = refs[13:19]
    wmix_ref, win_ref, wout_ref = _load_weights_once(refs[7:10], refs[19:])
    step = pl.program_id(0)
    n_slabs = D_RNN // LANES

    @pl.when(step == 0)
    def _():
        lru_scratch[-1][...] = jnp.zeros_like(lru_scratch[-1])
        for _ in _rglru_phases(xr0_ref, lambda d: jnp.zeros((SUBLANES, D_RNN), F32), gr0_ref,
                               lru_params, rec_ref.at[0], lru_scratch):
            pass

    def prev_rows(d):
        return jnp.concatenate(
            [xpn_ref[0, t, pl.ds(SUBLANES * (CONV_WIDTH - d), SUBLANES), :] for t in range(n_slabs)],
            axis=1)

    cur = step % 2
    next_rec = _rglru_phases(xrn_ref, prev_rows, grn_ref, lru_params,
                             rec_ref.at[1 - cur], lru_scratch)
    y = jnp.dot(attn_ref[...], wmix_ref[:D_ATTN, :], preferred_element_type=F32)
    y = y + jnp.dot(rec_ref[cur], wmix_ref[D_ATTN:, :], preferred_element_type=F32)
    x2 = _post_residual(x_ref[...], y, mod_ref, g_ref, 1, 1.0)
    o_ref[...] = _ffn(2, x2, mod_ref, g_ref, win_ref, wout_ref, acc_ref,
                      side_work=next_rec)


def _mixer_ffn_call(x, attn, xr, gr, mod9, g, lru_params, w_mix, w_in, w_out):
    n_tiles, n_slabs = xr.shape[0], xr.shape[1]
    row = pl.BlockSpec((ROW_TILE, D_MODEL), lambda i: (i, 0))
    half = pl.BlockSpec((ROW_TILE, D_ATTN), lambda i: (i, 0))
    nxt = lambda i: jnp.minimum(i + 1, n_tiles - 1)
    first_tile = pl.BlockSpec((1, n_slabs, ROW_TILE, LANES), lambda i: (0, 0, 0, 0),
                              pipeline_mode=pl.Buffered(1))
    next_tile = pl.BlockSpec((1, n_slabs, ROW_TILE, LANES), lambda i: (nxt(i), 0, 0, 0))
    last_rows = SUBLANES * CONV_WIDTH
    before_next = pl.BlockSpec(
        (1, n_slabs, last_rows, LANES),
        lambda i: (jnp.maximum(nxt(i) - 1, 0), 0, ROW_TILE // last_rows - 1, 0))
    weights = (w_mix, w_in, w_out)
    return pl.pallas_call(
        _mixer_ffn_kernel,
        out_shape=jax.ShapeDtypeStruct(x.shape, F32),
        grid=(n_tiles,),
        in_specs=[row, half, first_tile, first_tile, next_tile, before_next, next_tile,
                  _resident(mod9.shape), _resident(g.shape)]
                 + [_resident(p.shape) for p in lru_params]
                 + [pl.BlockSpec(memory_space=pl.ANY)] * len(weights),
        out_specs=row,
        scratch_shapes=[pltpu.VMEM((ROW_TILE, D_MODEL), F32),
                        pltpu.VMEM((2, ROW_TILE, D_RNN), BF16),
                        pltpu.VMEM((ROW_TILE + SUBLANES * (CONV_WIDTH - 1), D_RNN), F32),
                        pltpu.VMEM((ROW_TILE, D_RNN), F32),
                        pltpu.VMEM((ROW_TILE, D_RNN), F32),
                        pltpu.VMEM((ROW_TILE, D_RNN), F32),
                        pltpu.VMEM((n_slabs, ROW_TILE, LANES), F32),
                        pltpu.VMEM((SUBLANES, D_RNN), F32)]
                       + _weight_scratch(weights),
        compiler_params=pltpu.CompilerParams(
            dimension_semantics=("arbitrary",), vmem_limit_bytes=56 * MIB),
        name="mixer_ffn",
    )(x, attn, xr, gr, xr, xr, gr, mod9, g, *lru_params, w_mix, w_in, w_out)


def _blockdiag_halves(w):
    per_half = (D_RNN // 2) // LRU_BLOCK
    eye = jnp.eye(per_half, dtype=w.dtype)
    wh = w.reshape(2, per_half, LRU_BLOCK, LRU_BLOCK)
    dense = jnp.einsum('hgij,gk->hgikj', wh, eye)
    return dense.reshape(2, D_RNN // 2, D_RNN // 2).astype(BF16)


def kernel(x, c, w_ada, b_ada, norm_gain, ffn1_w_in, ffn1_w_out, mix_w_in, conv_w, conv_b,
           lru_w_x, lru_b_x, lru_w_a, lru_b_a, lru_lambda, mix_w_out, ffn2_w_in, ffn2_w_out):
    batch, seq, d = x.shape
    assert (batch, seq, d) == (1, SEQ, D_MODEL) and w_ada.shape[0] == 1
    xs = x.reshape(seq, d)
    g = norm_gain[0]

    mod = _mod_call(c.reshape(d, 1), w_ada[0], b_ada)
    mod9 = mod.reshape(9, d)

    x1, q, k, v, xr, gr = _ffn_inproj_call(
        xs, mod9, g, ffn1_w_in[0], ffn1_w_out[0], mix_w_in[0])
    attn = _attn_call(q, k, v)
    lru_params = (conv_w[0], conv_b, _blockdiag_halves(lru_w_x[0]),
                  _blockdiag_halves(lru_w_a[0]), lru_b_x, lru_b_a, lru_lambda)
    x3 = _mixer_ffn_call(x1, attn, xr, gr, mod9, g, lru_params, mix_w_out[0],
                         ffn2_w_in[0], ffn2_w_out[0])
    return x3.reshape(batch, seq, d)
```

```python
import math

import jax
import jax.numpy as jnp
from jax import lax
from jax.experimental import pallas as pl
from jax.experimental.pallas import tpu as pltpu

F32 = jnp.float32
BF16 = jnp.bfloat16

D_MODEL = 1024
SEQ = 16384
HEAD_DIM = 64
D_ATTN = 512
D_RNN = 512
LRU_BLOCK = 64
N_BACK = 128
ATTN_BLOCK = 128
CONV_WIDTH = 4
LRU_C = 8.0
D_FF = 2816
FFN_RES_WEIGHT = 0.5
RMS_EPS = 1e-6

MIB = 1024 * 1024
SUBLANES = 8
ROW_TILE = 512
SEG_LEN = ROW_TILE // SUBLANES
LRU_PARTS = 4
FF_CHUNK = 256
ATTN_CHUNK = 2048
LANES = 128
STAGE_ROWS, STAGE_COLS = 256, 512
STAGE_SLOTS = 12
PAIR = 2 * HEAD_DIM
MOD_TILE = 2304
ATTN_GROUP = 4


def _resident(shape):
    return pl.BlockSpec(shape, lambda *_: (0,) * len(shape),
                        pipeline_mode=pl.Buffered(1))


def _weight_scratch(weights):
    for w in weights:
        assert w.shape[0] % STAGE_ROWS == 0 and w.shape[1] % STAGE_COLS == 0, w.shape
    return ([pltpu.VMEM(w.shape, BF16) for w in weights]
            + [pltpu.VMEM((STAGE_SLOTS, STAGE_ROWS, STAGE_COLS), F32),
               pltpu.SemaphoreType.DMA((STAGE_SLOTS,))])


def _load_weights_once(hbm_refs, scratch_refs):
    *dst_refs, stage_ref, sem = scratch_refs
    blocks = [(hbm, dst, r, c)
              for hbm, dst in zip(hbm_refs, dst_refs)
              for r in range(0, dst.shape[0], STAGE_ROWS)
              for c in range(0, dst.shape[1], STAGE_COLS)]

    def copy(i):
        hbm, _, r, c = blocks[i]
        slot = i % STAGE_SLOTS
        return pltpu.make_async_copy(
            hbm.at[pl.ds(r, STAGE_ROWS), pl.ds(c, STAGE_COLS)],
            stage_ref.at[slot], sem.at[slot])

    @pl.when(pl.program_id(0) == 0)
    def _():
        for i in range(min(STAGE_SLOTS, len(blocks))):
            copy(i).start()
        for i, (_, dst, r, c) in enumerate(blocks):
            copy(i).wait()
            dst[r:r + STAGE_ROWS, c:c + STAGE_COLS] = (
                stage_ref[i % STAGE_SLOTS].astype(BF16))
            if i + STAGE_SLOTS < len(blocks):
                copy(i + STAGE_SLOTS).start()

    return dst_refs


def _rms(v):
    return v * lax.rsqrt(jnp.mean(v * v, axis=-1, keepdims=True) + RMS_EPS)


def _pre(x, mod_ref, g_ref, sub):
    shift = mod_ref[3 * sub:3 * sub + 1, :]
    scale = mod_ref[3 * sub + 1:3 * sub + 2, :]
    gain = g_ref[2 * sub:2 * sub + 1, :] * (1.0 + scale)
    return _rms(x) * gain + shift


def _post_residual(x, y, mod_ref, g_ref, sub, weight):
    gate = mod_ref[3 * sub + 2:3 * sub + 3, :]
    gain = weight * (1.0 + gate) * g_ref[2 * sub + 1:2 * sub + 2, :]
    return x + _rms(y) * gain


def _mod_kernel(c_ref, w_ref, b_ref, o_ref):
    c = c_ref[...]
    s = c * jax.nn.sigmoid(c)
    o_ref[...] = jnp.sum(s * w_ref[...], axis=0, keepdims=True) + b_ref[...]


def _mod_call(c_col, w_ada, b_ada):
    n = w_ada.shape[1]
    return pl.pallas_call(
        _mod_kernel,
        out_shape=jax.ShapeDtypeStruct((1, n), F32),
        grid=(n // MOD_TILE,),
        in_specs=[
            pl.BlockSpec((D_MODEL, 1), lambda j: (0, 0)),
            pl.BlockSpec((D_MODEL, MOD_TILE), lambda j: (0, j)),
            pl.BlockSpec((1, MOD_TILE), lambda j: (0, j)),
        ],
        out_specs=pl.BlockSpec((1, MOD_TILE), lambda j: (0, j)),
        compiler_params=pltpu.CompilerParams(
            dimension_semantics=("arbitrary",), vmem_limit_bytes=24 * MIB),
        name="mod",
    )(c_col, w_ada, b_ada)


def _ffn(sub, x, mod_ref, g_ref, win_ref, wout_ref, acc_ref, side_work=()):
    side_work = iter(side_work)
    h = _pre(x, mod_ref, g_ref, sub).astype(BF16)
    for lo in range(0, D_FF, FF_CHUNK):
        hi = min(lo + FF_CHUNK, D_FF)
        hg = jnp.dot(h, win_ref[:, lo:hi], preferred_element_type=F32)
        hu = jnp.dot(h, win_ref[:, D_FF + lo:D_FF + hi], preferred_element_type=F32)
        a = (hg * jax.nn.sigmoid(hg) * hu).astype(BF16)
        part = jnp.dot(a, wout_ref[lo:hi, :], preferred_element_type=F32)
        if lo == 0:
            acc_ref[...] = part
        else:
            acc_ref[...] += part
        for _ in range(2):
            next(side_work, None)
    for _ in side_work:
        pass
    return _post_residual(x, acc_ref[...], mod_ref, g_ref, sub, FFN_RES_WEIGHT)


def _ffn_inproj_kernel(x_ref, mod_ref, g_ref, *refs):
    x1_ref, q_ref, k_ref, v_ref, xr_ref, gr_ref, acc_ref = refs[3:10]
    win_ref, wout_ref, wmix_ref = _load_weights_once(refs[:3], refs[10:])
    x1 = _ffn(0, x_ref[...], mod_ref, g_ref, win_ref, wout_ref, acc_ref)
    x1_ref[...] = x1
    h = _pre(x1, mod_ref, g_ref, 1).astype(BF16)

    def proj(j, width):
        return jnp.dot(h, wmix_ref[:, j:j + width], preferred_element_type=F32)

    q_ref[...] = proj(0, D_ATTN) * (HEAD_DIM ** -0.5 * math.log2(math.e))
    k_ref[...] = proj(D_ATTN, D_ATTN)
    v_ref[...] = proj(2 * D_ATTN, D_ATTN)

    for n, dst in enumerate((xr_ref, gr_ref)):
        z = proj(3 * D_ATTN + n * D_RNN, D_RNN)
        for t in range(D_RNN // LANES):
            for s in range(SUBLANES):
                dst[0, t, pl.ds(s, SEG_LEN, stride=SUBLANES), :] = z[
                    s * SEG_LEN:(s + 1) * SEG_LEN, t * LANES:(t + 1) * LANES]


def _ffn_inproj_call(x, mod9, g, w_in, w_out, w_mix):
    s = x.shape[0]
    row = pl.BlockSpec((ROW_TILE, D_MODEL), lambda i: (i, 0))
    nat = pl.BlockSpec((ROW_TILE, D_ATTN), lambda i: (i, 0))
    s_nat = jax.ShapeDtypeStruct((s, D_ATTN), F32)
    seg = pl.BlockSpec((1, D_RNN // LANES, ROW_TILE, LANES), lambda i: (i, 0, 0, 0))
    s_seg = jax.ShapeDtypeStruct((s // ROW_TILE, D_RNN // LANES, ROW_TILE, LANES), F32)
    weights = (w_in, w_out, w_mix)
    return pl.pallas_call(
        _ffn_inproj_kernel,
        out_shape=[jax.ShapeDtypeStruct(x.shape, F32)] + [s_nat] * 3 + [s_seg] * 2,
        grid=(s // ROW_TILE,),
        in_specs=[row, _resident(mod9.shape), _resident(g.shape)]
                 + [pl.BlockSpec(memory_space=pl.ANY)] * len(weights),
        out_specs=[row] + [nat] * 3 + [seg] * 2,
        scratch_shapes=[pltpu.VMEM((ROW_TILE, D_MODEL), F32)]
                       + _weight_scratch(weights),
        compiler_params=pltpu.CompilerParams(
            dimension_semantics=("arbitrary",), vmem_limit_bytes=58 * MIB),
        name="ffn_inproj",
    )(x, mod9, g, w_in, w_out, w_mix)


def _cast_plan(weights, n_steps):
    nb = [(w.shape[0] // STAGE_ROWS) * (w.shape[1] // STAGE_COLS) for w in weights]
    for w in weights:
        assert w.shape[0] % STAGE_ROWS == 0 and w.shape[1] % STAGE_COLS == 0, w.shape
    mix, w_in, w_out = range(3)
    assert nb[w_in] == 2 * nb[w_out] and nb[w_out] + nb[mix] <= n_steps, nb
    return [(w_in, 0, 0, nb[w_out], 2, 0), (w_in, 1, 0, nb[w_out], 2, 1),
            (w_out, 2, 0, nb[w_out], 1, 0),
            (mix, 0, nb[w_out], nb[w_out] + nb[mix], 1, 0)]


def _cast_weights(step, n_steps, src_refs, dst_refs, stage_in, stage_out, sem_in, sem_out):
    plan = _cast_plan(src_refs, n_steps)
    windows = sorted({entry[2:4] for entry in plan})

    def copies(entry):
        w, slot, first, _, per_step, offset = entry
        n_cols = src_refs[w].shape[1] // STAGE_COLS
        b = (step - first) * per_step + offset
        r = pl.multiple_of(lax.div(b, jnp.int32(n_cols)) * STAGE_ROWS, STAGE_ROWS)
        c = pl.multiple_of(lax.rem(b, jnp.int32(n_cols)) * STAGE_COLS, STAGE_COLS)
        at = (pl.ds(r, STAGE_ROWS), pl.ds(c, STAGE_COLS))
        return (lambda: pltpu.make_async_copy(src_refs[w].at[at], stage_in.at[slot],
                                              sem_in.at[slot]),
                lambda: pltpu.make_async_copy(stage_out.at[slot], dst_refs[w].at[at],
                                              sem_out.at[slot]))

    def phase(body):
        def run():
            for window in windows:
                @pl.when((step >= window[0]) & (step < window[1]))
                def _():
                    for entry in plan:
                        if entry[2:4] == window:
                            body(entry[1], *copies(entry))
        return run

    def start(slot, read, write):
        read().start()

    def cast(slot, read, write):
        read().wait()
        stage_out[slot] = stage_in[slot].astype(BF16)
        write().start()

    def finish(slot, read, write):
        write().wait()

    return phase(start), phase(cast), phase(finish)


def _attn_kernel(q_ref, k_ref, v_ref, wmix_f32, win_f32, wout_f32,
                 o_ref, wmix_bf16, win_bf16, wout_bf16,
                 kc1, vc1, kc4, vc4, kc16, vc16, q4, q16, g4_ref,
                 bias_ref, s_ref, p_ref, max_ref, acc_ref, den_ref, nat_ref,
                 stage_in, stage_out, sem_in, sem_out):
    chunk = pl.program_id(1)
    n_chunks = SEQ // ATTN_CHUNK
    start_casts, do_casts, finish_casts = _cast_weights(
        pl.program_id(0) * n_chunks + chunk, (D_ATTN // PAIR) * n_chunks,
        (wmix_f32, win_f32, wout_f32), (wmix_bf16, win_bf16, wout_bf16),
        stage_in, stage_out, sem_in, sem_out)
    start_casts()
    nblk = ATTN_CHUNK // ATTN_BLOCK
    sub_rows = 4 * ATTN_BLOCK
    quarter = ATTN_BLOCK // 4
    lane = lax.broadcasted_iota(jnp.int32, (1, PAIR), 1)
    head_mask = [(lane < HEAD_DIM).astype(BF16), (lane >= HEAD_DIM).astype(BF16)]

    @pl.when((chunk == 0) & (pl.program_id(0) == 0))
    def _():
        for vc in (vc1, vc4, vc16):
            for h in range(2):
                vc[h, :, :, PAIR:] = jnp.broadcast_to(
                    head_mask[h], vc.shape[1:3] + (PAIR,))
        row = lax.broadcasted_iota(jnp.int32, (ATTN_BLOCK, 2 * ATTN_BLOCK), 0)
        col = lax.broadcasted_iota(jnp.int32, (ATTN_BLOCK, 2 * ATTN_BLOCK), 1)
        row4 = (row % quarter) * 4 + row // quarter
        for kind, r in enumerate((row, row4)):
            band = (col >= r) & (col <= r + N_BACK)
            bias_ref[2 * kind] = jnp.where(band, 0.0, -jnp.inf)
            bias_ref[2 * kind + 1] = jnp.where(band & (col >= ATTN_BLOCK), 0.0, -jnp.inf)

    prev = slice(0, ATTN_BLOCK)
    for kc, vc in ((kc1, vc1), (kc4, vc4), (kc16, vc16)):
        last = slice(kc.shape[1] - ATTN_BLOCK, kc.shape[1])

        @pl.when(chunk == 0)
        def _():
            kc[:, prev, :] = jnp.zeros((kc.shape[0], ATTN_BLOCK, PAIR), BF16)
            vc[:, :, prev, :PAIR] = jnp.zeros((2, vc.shape[1], ATTN_BLOCK, PAIR), BF16)

        @pl.when(chunk > 0)
        def _():
            kc[:, prev, :] = kc[:, last, :]
            vc[:, :, prev, :PAIR] = vc[:, :, last, :PAIR]

    def keep(what, g, k_dst, v_dst, q_dst):
        g = g.astype(BF16)
        if what == "q":
            q_dst(g)
        elif what == "k":
            k_dst(g)
        else:
            for h in range(2):
                v_dst(h, g * head_mask[h])

    for what, src in (("q", q_ref), ("k", k_ref), ("v", v_ref)):
        if what == "k":
            kc1[0, ATTN_BLOCK:, :] = src[...].astype(BF16)
        elif what == "v":
            nat = src[...].astype(BF16)
            for h in range(2):
                vc1[h, 0, ATTN_BLOCK:, :PAIR] = nat * head_mask[h]
        for i in range(ATTN_CHUNK // sub_rows):
            cur = slice((i + 1) * ATTN_BLOCK, (i + 2) * ATTN_BLOCK)
            for r4 in range(4):
                g = src[pl.ds(i * sub_rows + r4, ATTN_BLOCK, stride=4), :]
                g4_ref[r4, i * ATTN_BLOCK:(i + 1) * ATTN_BLOCK, :] = g

                def q_dst(b, i=i, r4=r4):
                    q4[i, r4] = b

                def k_dst(b, cur=cur, r4=r4):
                    kc4[r4, cur, :] = b

                def v_dst(h, b, cur=cur, r4=r4):
                    vc4[h, r4, cur, :PAIR] = b

                keep(what, g, k_dst, v_dst, q_dst)
        for r4 in range(4):
            for e in range(4):
                r = r4 + 4 * e
                g = g4_ref[r4, pl.ds(e, ATTN_BLOCK, stride=4), :]

                def q_dst(b, r=r):
                    q16[r] = b

                def k_dst(b, r=r):
                    kc16[r, ATTN_BLOCK:, :] = b

                def v_dst(h, b, r=r):
                    vc16[h, r, ATTN_BLOCK:, :PAIR] = b

                keep(what, g, k_dst, v_dst, q_dst)

    max_ref[...] = jnp.full(max_ref.shape, -jnp.inf, F32)
    acc_ref[...] = jnp.zeros(acc_ref.shape, F32)
    den_ref[...] = jnp.zeros(den_ref.shape, F32)

    def window(j):
        return pl.ds(j * ATTN_BLOCK, 2 * ATTN_BLOCK)

    def dilation1(j):
        i, jj = divmod(j, 4)
        rows = slice(jj * quarter, (jj + 1) * quarter)
        q = jnp.concatenate([q4[i, r4, rows, :] for r4 in range(4)], axis=0)
        pieces = [(pl.ds(i * sub_rows + r4 * ATTN_BLOCK + jj * quarter, quarter),
                   slice(r4 * quarter, (r4 + 1) * quarter)) for r4 in range(4)]
        return (q, lambda: kc1[0, window(j), :], lambda h: vc1[h, 0, window(j), :],
                True if j > 0 else chunk > 0, 1, pieces)

    def dilation4(t):
        i, r4 = divmod(t, 4)
        pieces = [(pl.ds(i * sub_rows + r4 * ATTN_BLOCK, ATTN_BLOCK),
                   slice(0, ATTN_BLOCK))]
        return (q4[i, r4], lambda: kc4[r4, window(i), :],
                lambda h: vc4[h, r4, window(i), :],
                True if i > 0 else chunk > 0, 0, pieces)

    def dilation16(r):
        e, r4 = divmod(r, 4)
        pieces = [(pl.ds(i * sub_rows + r4 * ATTN_BLOCK + e, quarter, stride=4),
                   slice(i * quarter, (i + 1) * quarter)) for i in range(4)]
        return (q16[r], lambda: kc16[r], lambda h: vc16[h, r],
                chunk > 0, 0, pieces)

    makers = [dilation4] * nblk + [dilation16] * nblk + [dilation1] * nblk
    n_tiles = len(makers)

    def scores(t):
        q, keys, _, has_prev, kind, pieces = makers[t](t % nblk)
        no_prev = 0 if has_prev is True else jnp.where(has_prev, 0, 1)
        bias = bias_ref[2 * kind + no_prev]
        kk = keys()
        for h in range(2):
            s = lax.dot_general(q * head_mask[h], kk, (((1,), (1,)), ((), ())),
                                preferred_element_type=F32) + bias
            s_ref[t, h] = s
            m = jnp.broadcast_to(jnp.max(s, axis=-1, keepdims=True),
                                 (ATTN_BLOCK, PAIR))
            for canon, rows in pieces:
                max_ref[h, canon, :] = jnp.maximum(max_ref[h, canon, :], m[rows])

    def softmax(t):
        pieces = makers[t](t % nblk)[5]
        for h in range(2):
            m = jnp.concatenate([max_ref[h, canon, :] for canon, _ in pieces], axis=0)
            p = jnp.exp2(s_ref[t, h] - jnp.concatenate([m, m], axis=1))
            p_ref[t % nblk, h] = p.astype(BF16)

    def values(t):
        _, _, vals, _, _, pieces = makers[t](t % nblk)
        pv = (jnp.dot(p_ref[t % nblk, 0], vals(0), preferred_element_type=F32)
              + jnp.dot(p_ref[t % nblk, 1], vals(1), preferred_element_type=F32))
        for canon, rows in pieces:
            acc_ref[canon, :] += pv[rows, :PAIR]
            den_ref[canon, :] += pv[rows, PAIR:]

    def pipeline(stages):
        n_groups = n_tiles // ATTN_GROUP
        for i in range(n_groups + len(stages) - 1):
            for lag in reversed(range(len(stages))):
                if 0 <= i - lag < n_groups:
                    for j in range(ATTN_GROUP):
                        stages[lag]((i - lag) * ATTN_GROUP + j)

    pipeline([scores])
    do_casts()
    pipeline([softmax, values])

    for i in range(ATTN_CHUNK // sub_rows):
        for r4 in range(4):
            canon = pl.ds(i * sub_rows + r4 * ATTN_BLOCK, ATTN_BLOCK)
            nat_ref[pl.ds(i * sub_rows + r4, ATTN_BLOCK, stride=4), :] = (
                acc_ref[canon, :] / den_ref[canon, :])
    o_ref[...] = nat_ref[...].astype(BF16)
    finish_casts()


def _attn_call(q, k, v, weights):
    s = q.shape[0]
    assert s == SEQ
    any_space = pl.BlockSpec(memory_space=pl.ANY)
    n_pairs = D_ATTN // PAIR
    blocks_per_chunk = ATTN_CHUNK // ATTN_BLOCK
    sub_per_chunk = ATTN_CHUNK // (4 * ATTN_BLOCK)
    nat = pl.BlockSpec((ATTN_CHUNK, PAIR), lambda p, c: (c, p))
    kcat = lambda classes, blocks: pltpu.VMEM(
        (classes, (blocks + 1) * ATTN_BLOCK, PAIR), BF16)
    vcat = lambda classes, blocks: pltpu.VMEM(
        (2, classes, (blocks + 1) * ATTN_BLOCK, 2 * PAIR), BF16)
    rows = lambda: pltpu.VMEM((ATTN_CHUNK, PAIR), F32)
    return pl.pallas_call(
        _attn_kernel,
        out_shape=[jax.ShapeDtypeStruct((s, D_ATTN), BF16)]
                  + [jax.ShapeDtypeStruct(w.shape, BF16) for w in weights],
        grid=(n_pairs, s // ATTN_CHUNK),
        in_specs=[nat] * 3 + [any_space] * 3,
        out_specs=[nat] + [any_space] * 3,
        scratch_shapes=[
            kcat(1, 16), vcat(1, 16), kcat(4, 4), vcat(4, 4), kcat(16, 1), vcat(16, 1),
            pltpu.VMEM((sub_per_chunk, 4, ATTN_BLOCK, PAIR), BF16),
            pltpu.VMEM((16, ATTN_BLOCK, PAIR), BF16),
            pltpu.VMEM((4, ATTN_CHUNK // 4, PAIR), F32),
            pltpu.VMEM((4, ATTN_BLOCK, 2 * ATTN_BLOCK), F32),
            pltpu.VMEM((3 * blocks_per_chunk, 2, ATTN_BLOCK, 2 * ATTN_BLOCK), F32),
            pltpu.VMEM((blocks_per_chunk, 2, ATTN_BLOCK, 2 * ATTN_BLOCK), BF16),
            pltpu.VMEM((2, ATTN_CHUNK, PAIR), F32),
            rows(), rows(), rows(),
            pltpu.VMEM((3, STAGE_ROWS, STAGE_COLS), F32),
            pltpu.VMEM((3, STAGE_ROWS, STAGE_COLS), BF16),
            pltpu.SemaphoreType.DMA((3,)), pltpu.SemaphoreType.DMA((3,))],
        compiler_params=pltpu.CompilerParams(
            dimension_semantics=("arbitrary", "arbitrary"),
            vmem_limit_bytes=56 * MIB),
        name="attn",
    )(q, k, v, *weights)


def _rglru_phases(xr_ref, prev_rows, gr_ref, params, rec_ref, scratch):
    cw_ref, cb_ref, wx_ref, wa_ref, bx_ref, ba_ref, lam_ref = params
    xcat_ref, a_ref, b_ref, h_ref, nat_ref, carry_ref = scratch
    half = D_RNN // 2
    n_slabs = D_RNN // LANES
    halo = SUBLANES * (CONV_WIDTH - 1)
    part = ROW_TILE // LRU_PARTS

    def wide(ref, rows):
        return jnp.concatenate([ref[0, t, rows, :] for t in range(n_slabs)], axis=1)

    sub = lax.broadcasted_iota(jnp.int32, (SUBLANES, D_RNN), 0)
    for d in range(1, CONV_WIDTH):
        cur = wide(xr_ref, pl.ds(SUBLANES * (SEG_LEN - d), 8))
        xcat_ref[pl.ds(halo - SUBLANES * d, SUBLANES), :] = jnp.where(
            sub == 0, pltpu.roll(prev_rows(d), 1, 0), pltpu.roll(cur, 1, 0))
    xcat_ref[halo:, :] = wide(xr_ref, slice(None))
    yield

    neg_lam = -lam_ref[...]
    softplus = jnp.maximum(neg_lam, 0.0) + jnp.log1p(jnp.exp(-jnp.abs(neg_lam)))
    seg_end = jnp.zeros((SUBLANES, D_RNN), F32)
    seg_prod = jnp.ones((SUBLANES, D_RNN), F32)
    for p in range(LRU_PARTS):
        rows = pl.ds(p * part, part)
        u = cb_ref[...]
        for k in range(CONV_WIDTH):
            d = CONV_WIDTH - 1 - k
            u = u + cw_ref[k:k + 1, :] * xcat_ref[pl.ds(halo - SUBLANES * d + p * part, part), :]
        ub = u.astype(BF16)

        def gate(w_ref, b_ref):
            lo = jnp.dot(ub[:, :half], w_ref[0], preferred_element_type=F32)
            hi = jnp.dot(ub[:, half:], w_ref[1], preferred_element_type=F32)
            z = jnp.concatenate([lo, hi], axis=1) + b_ref[...]
            return 0.5 * jnp.tanh(0.5 * z) + 0.5

        gate_x = gate(wx_ref, bx_ref)
        gate_a = gate(wa_ref, ba_ref)
        log_a = -LRU_C * gate_a * softplus
        a = jnp.exp(log_a)
        one_minus_a2 = -jnp.tanh(log_a) * (a * a + 1.0)
        root = jnp.where(one_minus_a2 > 0.0,
                         one_minus_a2 * lax.rsqrt(one_minus_a2), 0.0)
        a_ref[rows, :] = a
        b_ref[rows, :] = u * gate_x * root
        yield

        for j in range(p * part // SUBLANES, (p + 1) * part // SUBLANES):
            step_rows = pl.ds(j * SUBLANES, SUBLANES)
            a_j = a_ref[step_rows, :]
            seg_end = a_j * seg_end + b_ref[step_rows, :]
            seg_prod = a_j * seg_prod
            h_ref[step_rows, :] = seg_end
            a_ref[step_rows, :] = seg_prod
        yield

    entering = [carry_ref[0:1, :]]
    for s in range(SUBLANES):
        entering.append(seg_end[s:s + 1, :] + seg_prod[s:s + 1, :] * entering[s])
    carry_ref[0:1, :] = entering[SUBLANES]
    state_in = jnp.tile(jnp.concatenate(entering[:SUBLANES], axis=0),
                        (part // SUBLANES, 1))

    for p in range(LRU_PARTS):
        rows = pl.ds(p * part, part)
        g = wide(gr_ref, rows)
        g_half = 0.5 * g
        inner = g * (math.sqrt(2.0 / math.pi)
                     + (0.044715 * math.sqrt(2.0 / math.pi)) * (g * g))
        gelu = g_half + g_half * jnp.tanh(inner)
        out = (h_ref[rows, :] + a_ref[rows, :] * state_in) * gelu
        for t in range(n_slabs):
            nat_ref[t, rows, :] = out[:, t * LANES:(t + 1) * LANES]
        yield

    for t in range(n_slabs):
        for s in range(SUBLANES):
            rec_ref[s * SEG_LEN:(s + 1) * SEG_LEN, t * LANES:(t + 1) * LANES] = (
                nat_ref[t, pl.ds(s, SEG_LEN, stride=SUBLANES), :].astype(BF16))
        yield


def _mixer_ffn_kernel(x_ref, attn_ref, xr0_ref, gr0_ref, xrn_ref, xpn_ref, grn_ref,
                      mod_ref, g_ref, *refs):
    lru_params = refs[:7]
    o_ref, acc_ref, rec_ref = refs[10:13]
    lru_scratch = refs[13:19]
    wmix_ref, win_ref, wout_ref = w_refs = refs[19:22]
    step = pl.program_id(0)
    n_slabs = D_RNN // LANES

    @pl.when(step == 0)
    def _():
        fetches = [pltpu.make_async_copy(src, dst, refs[22].at[i])
                   for i, (src, dst) in enumerate(zip(refs[7:10], w_refs))]
        for fetch in fetches:
            fetch.start()
        lru_scratch[-1][...] = jnp.zeros_like(lru_scratch[-1])
        for _ in _rglru_phases(xr0_ref, lambda d: jnp.zeros((SUBLANES, D_RNN), F32), gr0_ref,
                               lru_params, rec_ref.at[0], lru_scratch):
            pass
        for fetch in fetches:
            fetch.wait()

    def prev_rows(d):
        return jnp.concatenate(
            [xpn_ref[0, t, pl.ds(SUBLANES * (CONV_WIDTH - d), SUBLANES), :] for t in range(n_slabs)],
            axis=1)

    cur = step % 2
    next_rec = _rglru_phases(xrn_ref, prev_rows, grn_ref, lru_params,
                             rec_ref.at[1 - cur], lru_scratch)
    y = jnp.dot(attn_ref[...], wmix_ref[:D_ATTN, :], preferred_element_type=F32)
    y = y + jnp.dot(rec_ref[cur], wmix_ref[D_ATTN:, :], preferred_element_type=F32)
    x2 = _post_residual(x_ref[...], y, mod_ref, g_ref, 1, 1.0)
    o_ref[...] = _ffn(2, x2, mod_ref, g_ref, win_ref, wout_ref, acc_ref,
                      side_work=next_rec)


def _mixer_ffn_call(x, attn, xr, gr, mod9, g, lru_params, w_mix, w_in, w_out):
    n_tiles, n_slabs = xr.shape[0], xr.shape[1]
    row = pl.BlockSpec((ROW_TILE, D_MODEL), lambda i: (i, 0))
    half = pl.BlockSpec((ROW_TILE, D_ATTN), lambda i: (i, 0))
    nxt = lambda i: jnp.minimum(i + 1, n_tiles - 1)
    first_tile = pl.BlockSpec((1, n_slabs, ROW_TILE, LANES), lambda i: (0, 0, 0, 0),
                              pipeline_mode=pl.Buffered(1))
    next_tile = pl.BlockSpec((1, n_slabs, ROW_TILE, LANES), lambda i: (nxt(i), 0, 0, 0))
    last_rows = SUBLANES * CONV_WIDTH
    before_next = pl.BlockSpec(
        (1, n_slabs, last_rows, LANES),
        lambda i: (jnp.maximum(nxt(i) - 1, 0), 0, ROW_TILE // last_rows - 1, 0))
    weights = (w_mix, w_in, w_out)
    return pl.pallas_call(
        _mixer_ffn_kernel,
        out_shape=jax.ShapeDtypeStruct(x.shape, F32),
        grid=(n_tiles,),
        in_specs=[row, half, first_tile, first_tile, next_tile, before_next, next_tile,
                  _resident(mod9.shape), _resident(g.shape)]
                 + [_resident(p.shape) for p in lru_params]
                 + [pl.BlockSpec(memory_space=pl.ANY)] * len(weights),
        out_specs=row,
        scratch_shapes=[pltpu.VMEM((ROW_TILE, D_MODEL), F32),
                        pltpu.VMEM((2, ROW_TILE, D_RNN), BF16),
                        pltpu.VMEM((ROW_TILE + SUBLANES * (CONV_WIDTH - 1), D_RNN), F32),
                        pltpu.VMEM((ROW_TILE, D_RNN), F32),
                        pltpu.VMEM((ROW_TILE, D_RNN), F32),
                        pltpu.VMEM((ROW_TILE, D_RNN), F32),
                        pltpu.VMEM((n_slabs, ROW_TILE, LANES), F32),
                        pltpu.VMEM((SUBLANES, D_RNN), F32)]
                       + [pltpu.VMEM(w.shape, BF16) for w in weights]
                       + [pltpu.SemaphoreType.DMA((len(weights),))],
        compiler_params=pltpu.CompilerParams(
            dimension_semantics=("arbitrary",), vmem_limit_bytes=56 * MIB),
        name="mixer_ffn",
    )(x, attn, xr, gr, xr, xr, gr, mod9, g, *lru_params, w_mix, w_in, w_out)


def _blockdiag_halves(w):
    per_half = (D_RNN // 2) // LRU_BLOCK
    eye = jnp.eye(per_half, dtype=w.dtype)
    wh = w.reshape(2, per_half, LRU_BLOCK, LRU_BLOCK)
    dense = jnp.einsum('hgij,gk->hgikj', wh, eye)
    return dense.reshape(2, D_RNN // 2, D_RNN // 2).astype(BF16)


def kernel(x, c, w_ada, b_ada, norm_gain, ffn1_w_in, ffn1_w_out, mix_w_in, conv_w, conv_b,
           lru_w_x, lru_b_x, lru_w_a, lru_b_a, lru_lambda, mix_w_out, ffn2_w_in, ffn2_w_out):
    batch, seq, d = x.shape
    assert (batch, seq, d) == (1, SEQ, D_MODEL) and w_ada.shape[0] == 1
    xs = x.reshape(seq, d)
    g = norm_gain[0]

    mod = _mod_call(c.reshape(d, 1), w_ada[0], b_ada)
    mod9 = mod.reshape(9, d)

    x1, q, k, v, xr, gr = _ffn_inproj_call(
        xs, mod9, g, ffn1_w_in[0], ffn1_w_out[0], mix_w_in[0])
    attn, w_mix, w_in, w_out = _attn_call(
        q, k, v, (mix_w_out[0], ffn2_w_in[0], ffn2_w_out[0]))
    lru_params = (conv_w[0], conv_b, _blockdiag_halves(lru_w_x[0]),
                  _blockdiag_halves(lru_w_a[0]), lru_b_x, lru_b_a, lru_lambda)
    x3 = _mixer_ffn_call(x1, attn, xr, gr, mod9, g, lru_params, w_mix, w_in, w_out)
    return x3.reshape(batch, seq, d)
```
